```python
import math
import jax, jax.numpy as jnp
from jax import lax
import numpy as np

D_MODEL = 2048
BATCH = 4
SEQ = 4096
DEPTH = 1

NORM_EPS = 1e-6
SSM_EXPAND = 2
SSM_D_INNER = SSM_EXPAND * D_MODEL
SSM_HEAD_DIM = 64
SSM_N_HEADS = SSM_D_INNER // SSM_HEAD_DIM
SSM_N_GROUPS = 8
SSM_HEADS_PER_GROUP = SSM_N_HEADS // SSM_N_GROUPS
SSM_D_STATE = 128
SSM_CONV = 4
SSM_CHUNK = 64
SSM_XBC_DIM = SSM_D_INNER + 2 * SSM_N_GROUPS * SSM_D_STATE
GDN_HEAD_DIM = 128
GDN_N_QK_HEADS = D_MODEL // GDN_HEAD_DIM
GDN_N_V_HEADS = 2 * GDN_N_QK_HEADS
GDN_V_PER_QK = GDN_N_V_HEADS // GDN_N_QK_HEADS
GDN_QK_DIM = GDN_N_QK_HEADS * GDN_HEAD_DIM
GDN_V_DIM = GDN_N_V_HEADS * GDN_HEAD_DIM
GDN_QKV_DIM = 2 * GDN_QK_DIM + GDN_V_DIM
GDN_CONV = 4
GDN_CHUNK = 64
FFN_HIDDEN = ((8 * D_MODEL + 3 * 256 - 1) // (3 * 256)) * 256
DT_MIN = 1e-3
DT_MAX = 1e-1
IN_SIZES = (SSM_D_INNER, SSM_XBC_DIM, SSM_N_HEADS, GDN_QKV_DIM, GDN_V_DIM, GDN_N_V_HEADS, GDN_N_V_HEADS, D_MODEL, D_MODEL)
IN_DIM = sum(IN_SIZES)
IN_SPLITS = [int(v) for v in np.cumsum(IN_SIZES)[:-1]]

kernel_name = 'hybrid_ssd_gdn_adaln_block'


def rms_norm(x, w, eps=NORM_EPS):
    xf = x.astype(jnp.float32)
    y = xf * lax.rsqrt(jnp.mean(xf * xf, axis=-1, keepdims=True) + eps)
    return (y * w.astype(jnp.float32)).astype(x.dtype)


def l2_normalize(x, eps=1e-6):
    return x * lax.rsqrt(jnp.sum(x * x, axis=-1, keepdims=True) + eps)


def modulate(h, shift, scale):
    return h * (1.0 + scale[:, None, :]) + shift[:, None, :]


def causal_depthwise_conv(x, w, b=None):
    k, ch = w.shape
    y = lax.conv_general_dilated(x, w[:, None, :].astype(x.dtype), window_strides=(1,), padding=[(k - 1, 0)], dimension_numbers=('NWC', 'WIO', 'NWC'), feature_group_count=ch)
    if b is not None:
        y = y + b.astype(x.dtype)
    return y


def mamba2_ssd_mixer(z, xbc, dt_raw, conv_w, conv_b, dt_bias, a_log, d_skip, norm_w):
    f32 = jnp.float32
    bsz, seq, _ = z.shape
    L, G, J, P, N = SSM_CHUNK, SSM_N_GROUPS, SSM_HEADS_PER_GROUP, SSM_HEAD_DIM, SSM_D_STATE
    nc = seq // L
    xbc = jax.nn.silu(causal_depthwise_conv(xbc, conv_w, conv_b))
    xs, b_in, c_in = jnp.split(xbc, [SSM_D_INNER, SSM_D_INNER + G * N], axis=-1)
    x = xs.astype(f32).reshape(bsz, nc, L, G, J, P)
    bm = b_in.astype(f32).reshape(bsz, nc, L, G, N)
    cm = c_in.astype(f32).reshape(bsz, nc, L, G, N)
    dt = jax.nn.softplus(dt_raw.astype(f32) + dt_bias.astype(f32)).reshape(bsz, nc, L, G, J)
    a = -jnp.exp(a_log.astype(f32)).reshape(G, J)
    xdt = x * dt[..., None]
    a_cum = jnp.cumsum(dt * a, axis=2).transpose(0, 1, 3, 4, 2)
    causal = jnp.tril(jnp.ones((L, L), dtype=bool))
    seg = a_cum[..., :, None] - a_cum[..., None, :]
    decay = jnp.where(causal, jnp.exp(jnp.where(causal, seg, 0.0)), 0.0)
    cb = jnp.einsum('bclgn,bcsgn->bcgls', cm, bm)
    y_diag = jnp.einsum('bcgjls,bcsgjp->bclgjp', cb[:, :, :, None] * decay, xdt)
    to_end = jnp.exp(a_cum[..., -1:] - a_cum)
    xw = xdt * to_end.transpose(0, 1, 4, 2, 3)[..., None]

    def step(state, inp):
        c_k, b_k, xw_k, acum_k = inp
        y_off = jnp.einsum('blgn,bgjpn->blgjp', c_k, state) * jnp.exp(acum_k).transpose(0, 3, 1, 2)[..., None]
        state = state * jnp.exp(acum_k[..., -1])[..., None, None] + jnp.einsum('blgn,blgjp->bgjpn', b_k, xw_k)
        return state, y_off

    state0 = jnp.zeros((bsz, G, J, P, N), f32)
    _, y_off = lax.scan(step, state0, (cm.swapaxes(0, 1), bm.swapaxes(0, 1), xw.swapaxes(0, 1), a_cum.swapaxes(0, 1)))
    y = y_diag + y_off.swapaxes(0, 1) + d_skip.astype(f32).reshape(G, J, 1) * x
    y = y.reshape(bsz, seq, SSM_D_INNER) * jax.nn.silu(z.astype(f32))
    y = rms_norm(y.reshape(bsz, seq, G, SSM_D_INNER // G), norm_w.reshape(G, SSM_D_INNER // G))
    return y.reshape(bsz, seq, SSM_D_INNER).astype(z.dtype)


def gated_deltanet_mixer(qkv, z, beta_raw, a_raw, conv_w, a_log, dt_bias, norm_w):
    f32 = jnp.float32
    bsz, seq, _ = qkv.shape
    L, H, Dk, Dv = GDN_CHUNK, GDN_N_V_HEADS, GDN_HEAD_DIM, GDN_HEAD_DIM
    nc = seq // L
    qkv = jax.nn.silu(causal_depthwise_conv(qkv, conv_w))
    q, k, v = jnp.split(qkv.astype(f32), [GDN_QK_DIM, 2 * GDN_QK_DIM], axis=-1)
    q = l2_normalize(q.reshape(bsz, seq, GDN_N_QK_HEADS, Dk)) * (Dk ** -0.5)
    k = l2_normalize(k.reshape(bsz, seq, GDN_N_QK_HEADS, Dk))
    q = jnp.repeat(q, GDN_V_PER_QK, axis=2)
    k = jnp.repeat(k, GDN_V_PER_QK, axis=2)
    v = v.reshape(bsz, seq, H, Dv)
    beta = jax.nn.sigmoid(beta_raw.astype(f32))
    g = -jnp.exp(a_log.astype(f32)) * jax.nn.softplus(a_raw.astype(f32) + dt_bias.astype(f32))

    def chunks(t):
        return t.reshape(bsz, nc, L, H, -1).transpose(0, 3, 1, 2, 4)

    qc, kc, vc = chunks(q), chunks(k), chunks(v)
    bc = beta.reshape(bsz, nc, L, H).transpose(0, 3, 1, 2)
    gam = jnp.cumsum(g.reshape(bsz, nc, L, H).transpose(0, 3, 1, 2), axis=-1)
    incl = jnp.tril(jnp.ones((L, L), dtype=bool))
    strict = jnp.tril(jnp.ones((L, L), dtype=bool), k=-1)
    seg = gam[..., :, None] - gam[..., None, :]
    dec = jnp.where(incl, jnp.exp(jnp.where(incl, seg, 0.0)), 0.0)
    a_mat = jnp.where(strict, jnp.einsum('bhcld,bhcsd->bhcls', kc, kc) * dec * bc[..., None], 0.0)
    rhs = jnp.concatenate([vc * bc[..., None], kc * (bc * jnp.exp(gam))[..., None]], axis=-1)
    sol = lax.linalg.triangular_solve(a_mat + jnp.eye(L, dtype=f32), rhs, left_side=True, lower=True, unit_diagonal=True)
    u, w = sol[..., :Dv], sol[..., Dv:]
    qk = jnp.einsum('bhcld,bhcsd->bhcls', qc, kc) * dec
    q_dec = qc * jnp.exp(gam)[..., None]
    k_tail = kc * jnp.exp(gam[..., -1:] - gam)[..., None]
    g_last = jnp.exp(gam[..., -1])

    def step(state, inp):
        u_k, w_k, qk_k, qd_k, kt_k, gl_k = inp
        v_new = u_k - jnp.einsum('bhld,bhdv->bhlv', w_k, state)
        o = jnp.einsum('bhld,bhdv->bhlv', qd_k, state) + jnp.einsum('bhls,bhsv->bhlv', qk_k, v_new)
        state = state * gl_k[..., None, None] + jnp.einsum('bhld,bhlv->bhdv', kt_k, v_new)
        return state, o

    xs = tuple(jnp.moveaxis(t, 2, 0) for t in (u, w, qk, q_dec, k_tail, g_last))
    _, o = lax.scan(step, jnp.zeros((bsz, H, Dk, Dv), f32), xs)
    o = o.transpose(1, 0, 3, 2, 4).reshape(bsz, seq, H, Dv)
    o = rms_norm(o, norm_w) * jax.nn.silu(z.astype(f32).reshape(bsz, seq, H, Dv))
    return o.reshape(bsz, seq, GDN_V_DIM).astype(qkv.dtype)


def hybrid_mixer(h, w_in, ssm_conv_w, ssm_conv_b, ssm_dt_bias, ssm_a_log, ssm_d_skip, ssm_norm_w, gdn_conv_w, gdn_a_log, gdn_dt_bias, gdn_norm_w, w_ssm_proj, w_gdn_proj, w_o):
    ssm_z, ssm_xbc, ssm_dt, gdn_qkv, gdn_z, gdn_beta, gdn_a, gate_ssm, gate_gdn = jnp.split(h @ w_in, IN_SPLITS, axis=-1)
    y_ssm = mamba2_ssd_mixer(ssm_z, ssm_xbc, ssm_dt, ssm_conv_w, ssm_conv_b, ssm_dt_bias, ssm_a_log, ssm_d_skip, ssm_norm_w) @ w_ssm_proj
    y_gdn = gated_deltanet_mixer(gdn_qkv, gdn_z, gdn_beta, gdn_a, gdn_conv_w, gdn_a_log, gdn_dt_bias, gdn_norm_w) @ w_gdn_proj
    merged = jax.nn.sigmoid(gate_ssm) * y_ssm + jax.nn.sigmoid(gate_gdn) * y_gdn
    return merged @ w_o


def swiglu(h, w_gate_up, w_down):
    gate, up = jnp.split(h @ w_gate_up, 2, axis=-1)
    return (jax.nn.silu(gate) * up) @ w_down


def _normal(key, shape, scale):
    return jax.random.normal(key, shape, jnp.float32) * scale


def _dt_bias_init(key, shape):
    u = jax.random.uniform(key, shape, jnp.float32)
    dt = jnp.exp(u * (math.log(DT_MAX) - math.log(DT_MIN)) + math.log(DT_MIN))
    return dt + jnp.log(-jnp.expm1(-dt))


def setup_inputs(seed: int = 0) -> dict:
    key = jax.random.key(seed)
    ks = jax.random.split(key, 24)
    L = DEPTH
    return {
        'x': _normal(ks[0], (BATCH, SEQ, D_MODEL), 1.0),
        'c': _normal(ks[1], (BATCH, D_MODEL), 1.0),
        'w_ada': _normal(ks[2], (L, D_MODEL, 6 * D_MODEL), 0.5 * D_MODEL ** -0.5),
        'b_ada': _normal(ks[3], (L, 6 * D_MODEL), 0.01),
        'norm_mix_w': 1.0 + _normal(ks[4], (L, D_MODEL), 0.02),
        'w_in': _normal(ks[5], (L, D_MODEL, IN_DIM), D_MODEL ** -0.5),
        'ssm_conv_w': _normal(ks[6], (L, SSM_CONV, SSM_XBC_DIM), SSM_CONV ** -0.5),
        'ssm_conv_b': _normal(ks[7], (L, SSM_XBC_DIM), 0.01),
        'ssm_dt_bias': _dt_bias_init(ks[8], (L, SSM_N_HEADS)),
        'ssm_a_log': jnp.log(jax.random.uniform(ks[9], (L, SSM_N_HEADS), jnp.float32, 1.0, 16.0)),
        'ssm_d_skip': 1.0 + _normal(ks[10], (L, SSM_N_HEADS), 0.02),
        'ssm_norm_w': 1.0 + _normal(ks[11], (L, SSM_D_INNER), 0.02),
        'gdn_conv_w': _normal(ks[12], (L, GDN_CONV, GDN_QKV_DIM), GDN_CONV ** -0.5),
        'gdn_a_log': jnp.log(jax.random.uniform(ks[13], (L, GDN_N_V_HEADS), jnp.float32, 1.0, 16.0)),
        'gdn_dt_bias': _dt_bias_init(ks[14], (L, GDN_N_V_HEADS)),
        'gdn_norm_w': 1.0 + _normal(ks[15], (L, GDN_HEAD_DIM), 0.02),
        'w_ssm_proj': _normal(ks[16], (L, SSM_D_INNER, D_MODEL), SSM_D_INNER ** -0.5),
        'w_gdn_proj': _normal(ks[17], (L, GDN_V_DIM, D_MODEL), GDN_V_DIM ** -0.5),
        'w_o': _normal(ks[18], (L, D_MODEL, D_MODEL), D_MODEL ** -0.5),
        'norm_ffn_w': 1.0 + _normal(ks[19], (L, D_MODEL), 0.02),
        'w_gate_up': _normal(ks[20], (L, D_MODEL, 2 * FFN_HIDDEN), D_MODEL ** -0.5),
        'w_down': _normal(ks[21], (L, FFN_HIDDEN, D_MODEL), FFN_HIDDEN ** -0.5),
        'final_norm_w': 1.0 + _normal(ks[22], (D_MODEL,), 0.02),
    }


def reference(x, c, w_ada, b_ada, norm_mix_w, w_in, ssm_conv_w, ssm_conv_b, ssm_dt_bias, ssm_a_log, ssm_d_skip, ssm_norm_w, gdn_conv_w, gdn_a_log, gdn_dt_bias, gdn_norm_w, w_ssm_proj, w_gdn_proj, w_o, norm_ffn_w, w_gate_up, w_down, final_norm_w):
    c_act = jax.nn.silu(c)
    for layer in range(DEPTH):
        shift_m, scale_m, gate_m, shift_f, scale_f, gate_f = jnp.split(c_act @ w_ada[layer] + b_ada[layer], 6, axis=-1)
        h = modulate(rms_norm(x, norm_mix_w[layer]), shift_m, scale_m)
        mix = hybrid_mixer(h, w_in[layer], ssm_conv_w[layer], ssm_conv_b[layer], ssm_dt_bias[layer], ssm_a_log[layer], ssm_d_skip[layer], ssm_norm_w[layer], gdn_conv_w[layer], gdn_a_log[layer], gdn_dt_bias[layer], gdn_norm_w[layer], w_ssm_proj[layer], w_gdn_proj[layer], w_o[layer])
        x = x + gate_m[:, None, :] * mix
        h = modulate(rms_norm(x, norm_ffn_w[layer]), shift_f, scale_f)
        x = x + gate_f[:, None, :] * swiglu(h, w_gate_up[layer], w_down[layer])
    return rms_norm(x, final_norm_w)
```

```python
import functools

import jax
import jax.numpy as jnp
from jax import lax
from jax.experimental import pallas as pl
from jax.experimental.pallas import tpu as pltpu

F32 = jnp.float32
BF16 = jnp.bfloat16

NORM_EPS = 1e-6
L2_EPS = 1e-6
CONV_K = 4
HALO = 8
CHUNK = 64
SEQ_TILE = 256
SSM_HEAD_DIM = 64
SSM_N_GROUPS = 8
SSM_D_STATE = 128
GDN_HEAD_DIM = 128
GDN_V_PER_QK = 2
SMALL_W = 128
VMEM_LIMIT = 56 * 1024 * 1024


def _cparams(sem):
    return pltpu.CompilerParams(dimension_semantics=sem, vmem_limit_bytes=VMEM_LIMIT)


def _bdot(a, b):
    return jnp.dot(a.astype(BF16), b.astype(BF16), preferred_element_type=F32)


def _bdot_nt(a, b):
    return lax.dot_general(a.astype(BF16), b.astype(BF16), (((1,), (1,)), ((), ())),
                           preferred_element_type=F32)


def _bdot_tn(a, b):
    return lax.dot_general(a.astype(BF16), b.astype(BF16), (((0,), (0,)), ((), ())),
                           preferred_element_type=F32)


def _split3(x):
    hi = x.astype(BF16)
    r1 = x - hi.astype(F32)
    mid = r1.astype(BF16)
    lo = (r1 - mid.astype(F32)).astype(BF16)
    return hi, mid, lo


def _dot01_left(m01, x):
    hi, mid, lo = _split3(x)
    d = lambda p: jnp.dot(m01, p, preferred_element_type=F32)
    return (d(lo) + d(mid)) + d(hi)


def _dot01_right(x, m01):
    hi, mid, lo = _split3(x)
    d = lambda p: jnp.dot(p, m01, preferred_element_type=F32)
    return (d(lo) + d(mid)) + d(hi)


def _dot01_nt(m01, x):
    hi, mid, lo = _split3(x)
    d = lambda p: lax.dot_general(m01, p, (((1,), (1,)), ((), ())), preferred_element_type=F32)
    return (d(lo) + d(mid)) + d(hi)


def _sigmoid(x):
    return 1.0 / (1.0 + jnp.exp(-x))


def _silu(x):
    return x * _sigmoid(x)


def _softplus(x):
    return jnp.maximum(x, 0.0) + jnp.log(1.0 + jnp.exp(-jnp.abs(x)))


def _rms(x, eps=NORM_EPS):
    return x * lax.rsqrt(jnp.mean(x * x, axis=-1, keepdims=True) + eps)


def _conv_silu(stage_ref, cur, w_ref, bias, rows):
    stage_ref[HALO:HALO + rows, :] = cur
    acc = None
    for k in range(CONV_K):
        start = HALO - (CONV_K - 1) + k
        term = stage_ref[start:start + rows, :] * w_ref[k:k + 1, :]
        acc = term if acc is None else acc + term
    if bias is not None:
        acc = acc + bias
    stage_ref[0:HALO, :] = cur[rows - HALO:rows, :]
    return _silu(acc)


def _div_pow2(v, p):
    return lax.shift_right_logical(v, jnp.int32(p.bit_length() - 1))


def _mod_pow2(v, p):
    return v & (p - 1)


def _chunk_masks(rows):
    r = lax.broadcasted_iota(jnp.int32, (rows, rows), 0)
    c = lax.broadcasted_iota(jnp.int32, (rows, rows), 1)
    same = _div_pow2(r, CHUNK) == _div_pow2(c, CHUNK)
    return same & (r >= c), same & (r > c)


def _ada_kernel(c_ref, w_ref, b_ref, o_ref):
    c_act = _silu(c_ref[...])
    o_ref[...] = _bdot(c_act, w_ref[...]) + b_ref[...]


def _adaln(c, w_ada, b_ada):
    bsz, d = c.shape
    n = w_ada.shape[1]
    tn = 1024
    return pl.pallas_call(
        _ada_kernel,
        grid=(n // tn,),
        in_specs=[pl.BlockSpec((bsz, d), lambda j: (0, 0)),
                  pl.BlockSpec((d, tn), lambda j: (0, j)),
                  pl.BlockSpec((1, tn), lambda j: (0, j))],
        out_specs=pl.BlockSpec((bsz, tn), lambda j: (0, j)),
        out_shape=jax.ShapeDtypeStruct((bsz, n), F32),
        compiler_params=_cparams(("arbitrary",)),
        name="adaln",
    )(c, w_ada, b_ada.reshape(1, n))


def _norm_mod_kernel(x_ref, w_ref, shift_ref, scale_ref, o_ref):
    y = _rms(x_ref[...]) * w_ref[...]
    o_ref[...] = (y * (1.0 + scale_ref[...]) + shift_ref[...]).astype(o_ref.dtype)


def _norm_mod(x, w, mod3, shift_idx, scale_idx):
    bsz, seq, d = x.shape
    tm = 512
    return pl.pallas_call(
        _norm_mod_kernel,
        grid=(bsz, seq // tm),
        in_specs=[pl.BlockSpec((None, tm, d), lambda b, i: (b, i, 0)),
                  pl.BlockSpec((1, d), lambda b, i: (0, 0)),
                  pl.BlockSpec((None, 1, d), lambda b, i: (b, 0, shift_idx)),
                  pl.BlockSpec((None, 1, d), lambda b, i: (b, 0, scale_idx))],
        out_specs=pl.BlockSpec((None, tm, d), lambda b, i: (b, i, 0)),
        out_shape=jax.ShapeDtypeStruct((bsz, seq, d), BF16),
        compiler_params=_cparams(("arbitrary", "arbitrary")),
        name="norm_mod",
    )(x, w.reshape(1, d), mod3, mod3)


def _mm_kernel(x_ref, w_ref, o_ref):
    o_ref[...] = jnp.dot(x_ref[...], w_ref[...], preferred_element_type=F32).astype(o_ref.dtype)


def _matmul(x, w, tm, tn, out_dtype, name):
    m, k = x.shape
    n = w.shape[1]
    return pl.pallas_call(
        _mm_kernel,
        grid=(n // tn, m // tm),
        in_specs=[pl.BlockSpec((tm, k), lambda j, i: (i, 0)),
                  pl.BlockSpec((k, tn), lambda j, i: (0, j))],
        out_specs=pl.BlockSpec((tm, tn), lambda j, i: (i, j)),
        out_shape=jax.ShapeDtypeStruct((m, n), out_dtype),
        compiler_params=_cparams(("arbitrary", "arbitrary")),
        name=name,
    )(x, w)


def _ssd_kernel(z_ref, x_ref, b_ref, c_ref, sm_ref, e_ref,
                cwx_ref, cwb_ref, cwc_ref, cbx_ref, cbb_ref, cbc_ref,
                dtb_ref, alog_ref, dsk_ref, nw_ref, o_ref,
                xs_ref, bs_ref, cs_ref, st_ref):
    rows = x_ref.shape[0]
    gw = x_ref.shape[1]
    n_heads = gw // SSM_HEAD_DIM

    @pl.when(pl.program_id(2) == 0)
    def _():
        xs_ref[0:HALO, :] = jnp.zeros((HALO, gw), F32)
        bs_ref[0:HALO, :] = jnp.zeros((HALO, SSM_D_STATE), F32)
        cs_ref[0:HALO, :] = jnp.zeros((HALO, SSM_D_STATE), F32)
        st_ref[...] = jnp.zeros_like(st_ref)

    xa = _conv_silu(xs_ref, x_ref[...], cwx_ref, cbx_ref[...], rows)
    ba = _conv_silu(bs_ref, b_ref[...], cwb_ref, cbb_ref[...], rows)
    ca = _conv_silu(cs_ref, c_ref[...], cwc_ref, cbc_ref[...], rows)

    incl, _ = _chunk_masks(rows)
    tril_bd = jnp.where(incl, 1.0, 0.0).astype(BF16)
    r = lax.broadcasted_iota(jnp.int32, (rows, gw), 0)
    l = lax.broadcasted_iota(jnp.int32, (rows, gw), 1)
    diag_t = _mod_pow2(r, CHUNK) == _mod_pow2(l, SSM_HEAD_DIM)
    causal_t = _mod_pow2(r, CHUNK) >= _mod_pow2(l, SSM_HEAD_DIM)
    ones_bd = jnp.where(_div_pow2(lax.broadcasted_iota(jnp.int32, (rows, rows), 0), CHUNK)
                        == _div_pow2(lax.broadcasted_iota(jnp.int32, (rows, rows), 1), CHUNK),
                        1.0, 0.0).astype(BF16)

    dt = _softplus(_dot01_right(sm_ref[...], e_ref[...]) + dtb_ref[...])
    da = dt * (-jnp.exp(alog_ref[...]))
    a_cum = _dot01_left(tril_bd, da)
    a_row = _dot01_left(ones_bd, jnp.where(diag_t, a_cum, 0.0))
    decay = jnp.exp(jnp.where(causal_t, a_cum - a_row, 0.0))
    decay = jnp.where(causal_t, decay, 0.0)

    xdt = xa * dt
    ea = jnp.exp(a_cum)
    hb = _div_pow2(lax.broadcasted_iota(jnp.int32, (gw, gw), 0), SSM_HEAD_DIM)
    hc = _div_pow2(lax.broadcasted_iota(jnp.int32, (gw, gw), 1), SSM_HEAD_DIM)
    head_bd = hb == hc

    ys = []
    for ci in range(rows // CHUNK):
        sl = slice(ci * CHUNK, (ci + 1) * CHUNK)
        b_c, c_c = ba[sl], ca[sl]
        ac = a_cum[sl]
        a_last = ac[CHUNK - 1:CHUNK, :]
        xdt_c = xdt[sl]
        cb = _bdot_nt(c_c, jnp.concatenate([b_c] * n_heads, axis=0))
        scores = cb * decay[sl]
        x_bd = jnp.where(head_bd, jnp.concatenate([xdt_c.astype(BF16)] * n_heads, axis=0),
                         jnp.zeros((), BF16))
        y_diag = jnp.dot(scores.astype(BF16), x_bd, preferred_element_type=F32)
        state = st_ref[...]
        y_off = _bdot(c_c, state) * ea[sl]
        xw = xdt_c * jnp.exp(a_last - ac)
        st_ref[...] = state * jnp.exp(a_last) + _bdot_tn(b_c, xw)
        ys.append(y_diag + y_off)
    y = jnp.concatenate(ys, axis=0) + dsk_ref[...] * xa
    y = y * _silu(z_ref[...])
    o_ref[...] = (_rms(y) * nw_ref[...]).astype(o_ref.dtype)


def _ssd(proj, small, e_sel, conv_w, conv_b, dtb_e, alog_e, dsk_e, norm_w, bsz, seq, offs):
    d_in = norm_w.shape[1]
    gw = d_in // SSM_N_GROUPS
    n = SSM_D_STATE
    ts = SEQ_TILE
    nt = seq // ts
    z0 = offs["ssm_z"] // gw
    x0 = offs["ssm_x"] // gw
    b0 = offs["ssm_b"] // n
    c0 = offs["ssm_c"] // n
    cb0 = d_in // n
    cc0 = (d_in + SSM_N_GROUPS * n) // n
    row = lambda b, g, i: b * nt + i
    return pl.pallas_call(
        _ssd_kernel,
        grid=(bsz, SSM_N_GROUPS, nt),
        in_specs=[
            pl.BlockSpec((ts, gw), lambda b, g, i: (row(b, g, i), z0 + g)),
            pl.BlockSpec((ts, gw), lambda b, g, i: (row(b, g, i), x0 + g)),
            pl.BlockSpec((ts, n), lambda b, g, i: (row(b, g, i), b0 + g)),
            pl.BlockSpec((ts, n), lambda b, g, i: (row(b, g, i), c0 + g)),
            pl.BlockSpec((ts, SMALL_W), lambda b, g, i: (row(b, g, i), 0)),
            pl.BlockSpec((None, SMALL_W, gw), lambda b, g, i: (g, 0, 0)),
            pl.BlockSpec((CONV_K, gw), lambda b, g, i: (0, g)),
            pl.BlockSpec((CONV_K, n), lambda b, g, i: (0, cb0 + g)),
            pl.BlockSpec((CONV_K, n), lambda b, g, i: (0, cc0 + g)),
            pl.BlockSpec((1, gw), lambda b, g, i: (0, g)),
            pl.BlockSpec((1, n), lambda b, g, i: (0, cb0 + g)),
            pl.BlockSpec((1, n), lambda b, g, i: (0, cc0 + g)),
            pl.BlockSpec((1, gw), lambda b, g, i: (0, g)),
            pl.BlockSpec((1, gw), lambda b, g, i: (0, g)),
            pl.BlockSpec((1, gw), lambda b, g, i: (0, g)),
            pl.BlockSpec((1, gw), lambda b, g, i: (0, g)),
        ],
        out_specs=pl.BlockSpec((ts, gw), lambda b, g, i: (row(b, g, i), g)),
        out_shape=jax.ShapeDtypeStruct((bsz * seq, d_in), BF16),
        scratch_shapes=[pltpu.VMEM((ts + HALO, gw), F32),
                        pltpu.VMEM((ts + HALO, n), F32),
                        pltpu.VMEM((ts + HALO, n), F32),
                        pltpu.VMEM((n, gw), F32)],
        compiler_params=_cparams(("arbitrary", "arbitrary", "arbitrary")),
        name="ssd_mixer",
    )(proj, proj, proj, proj, small, e_sel,
      conv_w, conv_w, conv_w, conv_b, conv_b, conv_b, dtb_e, alog_e, dsk_e, norm_w)


def _gdn_kernel(q_ref, k_ref, v_ref, z_ref, sm_ref, eb_ref, ea_ref,
                cwq_ref, cwk_ref, cwv_ref, alog_ref, dtb_ref, nw_ref, o_ref,
                qs_ref, ks_ref, vs_ref, st_ref):
    rows = q_ref.shape[0]
    dk = q_ref.shape[1]
    dv = GDN_HEAD_DIM
    n_v = v_ref.shape[1] // dv
    assert rows == 2 * dk

    @pl.when(pl.program_id(2) == 0)
    def _():
        qs_ref[0:HALO, :] = jnp.zeros((HALO, dk), F32)
        ks_ref[0:HALO, :] = jnp.zeros((HALO, dk), F32)
        vs_ref[0:HALO, :] = jnp.zeros((HALO, n_v * dv), F32)
        st_ref[...] = jnp.zeros_like(st_ref)

    q = _conv_silu(qs_ref, q_ref[...], cwq_ref, None, rows)
    k = _conv_silu(ks_ref, k_ref[...], cwk_ref, None, rows)
    v = _conv_silu(vs_ref, v_ref[...], cwv_ref, None, rows)
    q = q * lax.rsqrt(jnp.sum(q * q, axis=-1, keepdims=True) + L2_EPS) * (dk ** -0.5)
    k = k * lax.rsqrt(jnp.sum(k * k, axis=-1, keepdims=True) + L2_EPS)

    incl, strict = _chunk_masks(rows)
    tril_bd = jnp.where(incl, 1.0, 0.0).astype(BF16)
    lane0 = jnp.where(lax.broadcasted_iota(jnp.int32, (rows, dk), 1) == 0, 1.0, 0.0).astype(BF16)
    eye = jnp.where(lax.broadcasted_iota(jnp.int32, (rows, rows), 0)
                    == lax.broadcasted_iota(jnp.int32, (rows, rows), 1), 1.0, 0.0)

    sm = sm_ref[...]
    beta = _sigmoid(_dot01_right(sm, eb_ref[...]))
    g = -jnp.exp(alog_ref[...]) * _softplus(_dot01_right(sm, ea_ref[...]) + dtb_ref[...])
    gam = _dot01_left(tril_bd, g)

    kk = _bdot_nt(k, k)
    qk = _bdot_nt(q, k)
    z = z_ref[...]

    for h in range(n_v):
        hs = slice(h * dv, (h + 1) * dv)
        gb = gam[:, hs]
        bb = beta[:, hs]
        g_row = _dot01_nt(lane0, gb)
        g_col = jnp.concatenate([gb, gb], axis=1)
        b_col = jnp.concatenate([bb, bb], axis=1)
        dec = jnp.exp(jnp.where(incl, g_col - g_row, 0.0))
        a = jnp.where(strict, kk * dec, 0.0) * b_col
        p = eye - a
        pw = a
        n_sq = CHUNK.bit_length() - 2
        for _ in range(n_sq):
            pw = _bdot(pw, pw)
            p = p + _bdot(p, pw)
        eg = jnp.exp(gb)
        rhs = jnp.concatenate([v[:, hs] * bb, k * (bb * eg)], axis=1)
        sol = _bdot(p, rhs)
        u, w = sol[:, :dv], sol[:, dv:]
        qkm = jnp.where(incl, qk * dec, 0.0)
        qd = q * eg

        qs_parts, vn_parts = [], []
        for ci in range(rows // CHUNK):
            sl = slice(ci * CHUNK, (ci + 1) * CHUNK)
            state = st_ref[h]
            gl = gb[ci * CHUNK + CHUNK - 1:(ci + 1) * CHUNK, :]
            wq = _bdot(jnp.concatenate([w[sl], qd[sl]], axis=0), state)
            v_new = u[sl] - wq[:CHUNK]
            k_tail = k[sl] * jnp.exp(gl - gb[sl])
            st_ref[h] = state * jnp.exp(gl) + _bdot_tn(k_tail, v_new)
            qs_parts.append(wq[CHUNK:])
            vn_parts.append(v_new)
        o = jnp.concatenate(qs_parts, axis=0) + _bdot(qkm, jnp.concatenate(vn_parts, axis=0))
        o = _rms(o) * nw_ref[...] * _silu(z[:, hs])
        o_ref[:, hs] = o.astype(o_ref.dtype)


def _gdn(proj, small, e_beta, e_a, conv_w, alog_e, dtb_e, norm_w, bsz, seq, offs, n_qk):
    dk = GDN_HEAD_DIM
    vw = GDN_V_PER_QK * GDN_HEAD_DIM
    ts = SEQ_TILE
    nt = seq // ts
    q0 = offs["gdn_q"] // dk
    k0 = offs["gdn_k"] // dk
    v0 = offs["gdn_v"] // vw
    z0 = offs["gdn_z"] // vw
    ck0 = n_qk
    cv0 = (2 * n_qk * dk) // vw
    row = lambda b, g, i: b * nt + i
    return pl.pallas_call(
        _gdn_kernel,
        grid=(bsz, n_qk, nt),
        in_specs=[
            pl.BlockSpec((ts, dk), lambda b, g, i: (row(b, g, i), q0 + g)),
            pl.BlockSpec((ts, dk), lambda b, g, i: (row(b, g, i), k0 + g)),
            pl.BlockSpec((ts, vw), lambda b, g, i: (row(b, g, i), v0 + g)),
            pl.BlockSpec((ts, vw), lambda b, g, i: (row(b, g, i), z0 + g)),
            pl.BlockSpec((ts, SMALL_W), lambda b, g, i: (row(b, g, i), 0)),
            pl.BlockSpec((None, SMALL_W, vw), lambda b, g, i: (g, 0, 0)),
            pl.BlockSpec((None, SMALL_W, vw), lambda b, g, i: (g, 0, 0)),
            pl.BlockSpec((CONV_K, dk), lambda b, g, i: (0, g)),
            pl.BlockSpec((CONV_K, dk), lambda b, g, i: (0, ck0 + g)),
            pl.BlockSpec((CONV_K, vw), lambda b, g, i: (0, cv0 + g)),
            pl.BlockSpec((1, vw), lambda b, g, i: (0, g)),
            pl.BlockSpec((1, vw), lambda b, g, i: (0, g)),
            pl.BlockSpec((1, dk), lambda b, g, i: (0, 0)),
        ],
        out_specs=pl.BlockSpec((ts, vw), lambda b, g, i: (row(b, g, i), g)),
        out_shape=jax.ShapeDtypeStruct((bsz * seq, n_qk * vw), BF16),
        scratch_shapes=[pltpu.VMEM((ts + HALO, dk), F32),
                        pltpu.VMEM((ts + HALO, dk), F32),
                        pltpu.VMEM((ts + HALO, vw), F32),
                        pltpu.VMEM((GDN_V_PER_QK, dk, GDN_HEAD_DIM), F32)],
        compiler_params=_cparams(("arbitrary", "arbitrary", "arbitrary")),
        name="gdn_mixer",
    )(proj, proj, proj, proj, small, e_beta, e_a,
      conv_w, conv_w, conv_w, alog_e, dtb_e, norm_w)


def _merge_kernel(ys_ref, yg_ref, gs_ref, gg_ref, ws_ref, wg_ref, o_ref):
    a = jnp.dot(ys_ref[...], ws_ref[...], preferred_element_type=F32)
    b = jnp.dot(yg_ref[...], wg_ref[...], preferred_element_type=F32)
    o_ref[...] = (_sigmoid(gs_ref[...]) * a + _sigmoid(gg_ref[...]) * b).astype(o_ref.dtype)


def _merge(y_ssm, y_gdn, proj, w_ssm, w_gdn, offs):
    m, k = y_ssm.shape
    n = w_ssm.shape[1]
    tm, tn = 512, 512
    gs0 = offs["gate_ssm"] // tn
    gg0 = offs["gate_gdn"] // tn
    return pl.pallas_call(
        _merge_kernel,
        grid=(n // tn, m // tm),
        in_specs=[pl.BlockSpec((tm, k), lambda j, i: (i, 0)),
                  pl.BlockSpec((tm, k), lambda j, i: (i, 0)),
                  pl.BlockSpec((tm, tn), lambda j, i: (i, gs0 + j)),
                  pl.BlockSpec((tm, tn), lambda j, i: (i, gg0 + j)),
                  pl.BlockSpec((k, tn), lambda j, i: (0, j)),
                  pl.BlockSpec((k, tn), lambda j, i: (0, j))],
        out_specs=pl.BlockSpec((tm, tn), lambda j, i: (i, j)),
        out_shape=jax.ShapeDtypeStruct((m, n), BF16),
        compiler_params=_cparams(("arbitrary", "arbitrary")),
        name="merge_proj",
    )(y_ssm, y_gdn, proj, proj, w_ssm, w_gdn)


def _mixout_kernel(m_ref, wo_ref, x_ref, gate_ref, nw_ref, shift_ref, scale_ref, x1_ref, h2_ref):
    mix = jnp.dot(m_ref[...], wo_ref[...], preferred_element_type=F32)
    x1 = x_ref[...] + gate_ref[...] * mix
    x1_ref[...] = x1
    y = _rms(x1) * nw_ref[...]
    h2_ref[...] = (y * (1.0 + scale_ref[...]) + shift_ref[...]).astype(h2_ref.dtype)


def _mixout(merged, w_o, x, mod3, norm_w, gate_idx, shift_idx, scale_idx):
    bsz, seq, d = x.shape
    tm = 512
    nt = seq // tm
    return pl.pallas_call(
        _mixout_kernel,
        grid=(bsz, nt),
        in_specs=[pl.BlockSpec((tm, d), lambda b, i: (b * nt + i, 0)),
                  pl.BlockSpec((d, d), lambda b, i: (0, 0)),
                  pl.BlockSpec((None, tm, d), lambda b, i: (b, i, 0)),
                  pl.BlockSpec((None, 1, d), lambda b, i: (b, 0, gate_idx)),
                  pl.BlockSpec((1, d), lambda b, i: (0, 0)),
                  pl.BlockSpec((None, 1, d), lambda b, i: (b, 0, shift_idx)),
                  pl.BlockSpec((None, 1, d), lambda b, i: (b, 0, scale_idx))],
        out_specs=[pl.BlockSpec((None, tm, d), lambda b, i: (b, i, 0)),
                   pl.BlockSpec((None, tm, d), lambda b, i: (b, i, 0))],
        out_shape=[jax.ShapeDtypeStruct((bsz, seq, d), F32),
                   jax.ShapeDtypeStruct((bsz, seq, d), BF16)],
        compiler_params=_cparams(("arbitrary", "arbitrary")),
        name="mix_out",
    )(merged, w_o, x, mod3, norm_w.reshape(1, d), mod3, mod3)


def _gateup_kernel(h_ref, wg_ref, wu_ref, o_ref):
    h = h_ref[...]
    g = jnp.dot(h, wg_ref[...], preferred_element_type=F32)
    u = jnp.dot(h, wu_ref[...], preferred_element_type=F32)
    o_ref[...] = (_silu(g) * u).astype(o_ref.dtype)


def _gateup(h, w_gate_up):
    m, k = h.shape
    hid = w_gate_up.shape[1] // 2
    tm, tn = 1024, 512
    up0 = hid // tn
    return pl.pallas_call(
        _gateup_kernel,
        grid=(hid // tn, m // tm),
        in_specs=[pl.BlockSpec((tm, k), lambda j, i: (i, 0)),
                  pl.BlockSpec((k, tn), lambda j, i: (0, j)),
                  pl.BlockSpec((k, tn), lambda j, i: (0, up0 + j))],
        out_specs=pl.BlockSpec((tm, tn), lambda j, i: (i, j)),
        out_shape=jax.ShapeDtypeStruct((m, hid), BF16),
        compiler_params=_cparams(("arbitrary", "arbitrary")),
        name="ffn_gate_up",
    )(h, w_gate_up, w_gate_up)


def _down_kernel(a_ref, wd_ref, x1_ref, gate_ref, fw_ref, o_ref, *, final):
    ffn = jnp.dot(a_ref[...], wd_ref[...], preferred_element_type=F32)
    x2 = x1_ref[...] + gate_ref[...] * ffn
    if final:
        x2 = _rms(x2) * fw_ref[...]
    o_ref[...] = x2


def _down(act, w_down, x1, mod3, final_w, gate_idx, final):
    bsz, seq, d = x1.shape
    hid = act.shape[1]
    tm = 256
    nt = seq // tm
    return pl.pallas_call(
        functools.partial(_down_kernel, final=final),
        grid=(bsz, nt),
        in_specs=[pl.BlockSpec((tm, hid), lambda b, i: (b * nt + i, 0)),
                  pl.BlockSpec((hid, d), lambda b, i: (0, 0), pipeline_mode=pl.Buffered(1)),
                  pl.BlockSpec((None, tm, d), lambda b, i: (b, i, 0)),
                  pl.BlockSpec((None, 1, d), lambda b, i: (b, 0, gate_idx)),
                  pl.BlockSpec((1, d), lambda b, i: (0, 0))],
        out_specs=pl.BlockSpec((None, tm, d), lambda b, i: (b, i, 0)),
        out_shape=jax.ShapeDtypeStruct((bsz, seq, d), F32),
        compiler_params=_cparams(("arbitrary", "arbitrary")),
        name="ffn_down",
    )(act, w_down, x1, mod3, final_w.reshape(1, d))


def _selector(n_groups, src0, per_group, width, rep):
    k = jnp.arange(SMALL_W)[None, :, None]
    g = jnp.arange(n_groups)[:, None, None]
    c = jnp.arange(width)[None, None, :]
    return (k == src0 + g * per_group + c // rep).astype(BF16)


def kernel(x, c, w_ada, b_ada, norm_mix_w, w_in, ssm_conv_w, ssm_conv_b, ssm_dt_bias, ssm_a_log, ssm_d_skip, ssm_norm_w, gdn_conv_w, gdn_a_log, gdn_dt_bias, gdn_norm_w, w_ssm_proj, w_gdn_proj, w_o, norm_ffn_w, w_gate_up, w_down, final_norm_w):
    bsz, seq, d = x.shape
    depth = w_ada.shape[0]
    d_in = ssm_norm_w.shape[1]
    n_ssm_heads = ssm_dt_bias.shape[1]
    gn = SSM_N_GROUPS * SSM_D_STATE
    n_v = gdn_a_log.shape[1]
    n_qk = n_v // GDN_V_PER_QK
    qk_dim = n_qk * GDN_HEAD_DIM
    v_dim = n_v * GDN_HEAD_DIM
    assert n_ssm_heads + 2 * n_v == SMALL_W and seq % SEQ_TILE == 0

    sizes = (d_in, d_in + 2 * gn, n_ssm_heads, 2 * qk_dim + v_dim, v_dim, n_v, n_v, d, d)
    starts = [0]
    for s in sizes:
        starts.append(starts[-1] + s)
    (s_z, s_xbc, s_dt, s_qkv, s_gz, s_beta, s_a, s_gs, s_gg) = starts[:-1]
    offs = {"ssm_z": 0, "ssm_x": d_in, "ssm_b": 2 * d_in, "ssm_c": 2 * d_in + gn}
    o_qkv = 2 * d_in + 2 * gn
    offs.update({"gdn_q": o_qkv, "gdn_k": o_qkv + qk_dim, "gdn_v": o_qkv + 2 * qk_dim})
    offs["gdn_z"] = o_qkv + 2 * qk_dim + v_dim
    offs["gate_ssm"] = offs["gdn_z"] + v_dim
    offs["gate_gdn"] = offs["gate_ssm"] + d

    e_dt = _selector(SSM_N_GROUPS, 0, n_ssm_heads // SSM_N_GROUPS, d_in // SSM_N_GROUPS, SSM_HEAD_DIM)
    e_beta = _selector(n_qk, n_ssm_heads, GDN_V_PER_QK, GDN_V_PER_QK * GDN_HEAD_DIM, GDN_HEAD_DIM)
    e_a = _selector(n_qk, n_ssm_heads + n_v, GDN_V_PER_QK, GDN_V_PER_QK * GDN_HEAD_DIM, GDN_HEAD_DIM)

    c_pad = c
    out = x
    for layer in range(depth):
        wl = w_in[layer]
        w_main = jnp.concatenate([wl[:, s_z:s_dt], wl[:, s_qkv:s_beta], wl[:, s_gs:]], axis=1).astype(BF16)
        w_small = jnp.concatenate([wl[:, s_dt:s_qkv], wl[:, s_beta:s_gs]], axis=1).astype(BF16)

        mod = _adaln(c_pad, w_ada[layer], b_ada[layer])
        mod3 = mod.reshape(bsz, 1, 6 * d)
        h = _norm_mod(out, norm_mix_w[layer], mod3, 0, 1).reshape(bsz * seq, d)
        proj = _matmul(h, w_main, 1024, 1024, F32, "in_proj")
        small = _matmul(h, w_small, 1024, SMALL_W, F32, "in_proj_small")

        rep = lambda p: jnp.repeat(p, SSM_HEAD_DIM)[None, :]
        y_ssm = _ssd(proj, small, e_dt, ssm_conv_w[layer], ssm_conv_b[layer][None, :],
                     rep(ssm_dt_bias[layer]), rep(ssm_a_log[layer]), rep(ssm_d_skip[layer]),
                     ssm_norm_w[layer][None, :], bsz, seq, offs)
        repv = lambda p: jnp.repeat(p, GDN_HEAD_DIM)[None, :]
        y_gdn = _gdn(proj, small, e_beta, e_a, gdn_conv_w[layer], repv(gdn_a_log[layer]),
                     repv(gdn_dt_bias[layer]), gdn_norm_w[layer][None, :], bsz, seq, offs, n_qk)

        merged = _merge(y_ssm, y_gdn, proj, w_ssm_proj[layer].astype(BF16), w_gdn_proj[layer].astype(BF16), offs)
        x1, h2 = _mixout(merged, w_o[layer].astype(BF16), out, mod3, norm_ffn_w[layer], 2, 3, 4)
        act = _gateup(h2.reshape(bsz * seq, d), w_gate_up[layer].astype(BF16))
        out = _down(act, w_down[layer].astype(BF16), x1, mod3, final_norm_w, 5, layer == depth - 1)
    return out
```

```python
import functools

import jax
import jax.numpy as jnp
from jax import lax
from jax.experimental import pallas as pl
from jax.experimental.pallas import tpu as pltpu

F32 = jnp.float32
BF16 = jnp.bfloat16

NORM_EPS = 1e-6
L2_EPS = 1e-6
CONV_K = 4
SUBLANES = 8
HALO = 8
CHUNK = 64
INV_BASE = 16
SEQ_TILE = 256
SSM_HEAD_DIM = 64
SSM_N_GROUPS = 8
SSM_D_STATE = 128
GDN_HEAD_DIM = 128
GDN_V_PER_QK = 2
GDN_QK_PER_STEP = 4
SMALL_W = 128
VMEM_LIMIT = 56 * 1024 * 1024


def _cparams(sem):
    return pltpu.CompilerParams(dimension_semantics=sem, vmem_limit_bytes=VMEM_LIMIT)


def _bdot(a, b):
    return jnp.dot(a.astype(BF16), b.astype(BF16), preferred_element_type=F32)


def _bdot_nt(a, b):
    return lax.dot_general(a.astype(BF16), b.astype(BF16), (((1,), (1,)), ((), ())),
                           preferred_element_type=F32)


def _bdot_tn(a, b):
    return lax.dot_general(a.astype(BF16), b.astype(BF16), (((0,), (0,)), ((), ())),
                           preferred_element_type=F32)


def _split3(x):
    hi = x.astype(BF16)
    r1 = x - hi.astype(F32)
    mid = r1.astype(BF16)
    lo = (r1 - mid.astype(F32)).astype(BF16)
    return hi, mid, lo


def _dot01_left(m01, x):
    hi, mid, lo = _split3(x)
    d = lambda p: jnp.dot(m01, p, preferred_element_type=F32)
    return (d(lo) + d(mid)) + d(hi)


def _dot01_right(x, m01):
    hi, mid, lo = _split3(x)
    d = lambda p: jnp.dot(p, m01, preferred_element_type=F32)
    return (d(lo) + d(mid)) + d(hi)


def _dot01_tn(x, m01):
    hi, mid, lo = _split3(x)
    d = lambda p: lax.dot_general(p, m01, (((0,), (0,)), ((), ())), preferred_element_type=F32)
    return (d(lo) + d(mid)) + d(hi)


def _sigmoid(x):
    return 1.0 / (1.0 + jnp.exp(-x))


def _silu(x):
    return x * _sigmoid(x)


def _softplus(x):
    return jnp.maximum(x, 0.0) + jnp.log(1.0 + jnp.exp(-jnp.abs(x)))


def _rms(x, eps=NORM_EPS):
    return x * lax.rsqrt(jnp.mean(x * x, axis=-1, keepdims=True) + eps)


def _conv_silu(stage_ref, cur, w_ref, bias, rows):
    stage_ref[HALO:HALO + rows, :] = cur
    acc = None
    for k in reversed(range(CONV_K)):
        start = HALO - (CONV_K - 1) + k
        term = stage_ref[start:start + rows, :] * w_ref[k:k + 1, :]
        acc = term if acc is None else acc + term
    if bias is not None:
        acc = acc + bias
    stage_ref[0:HALO, :] = cur[rows - HALO:rows, :]
    return _silu(acc)


def _ada_kernel(c_ref, w_ref, b_ref, o_ref):
    c_act = _silu(c_ref[...])
    o_ref[...] = _bdot(c_act, w_ref[...]) + b_ref[...]


def _adaln(c, w_ada, b_ada):
    bsz, d = c.shape
    n = w_ada.shape[1]
    tn = 1024
    return pl.pallas_call(
        _ada_kernel,
        grid=(n // tn,),
        in_specs=[pl.BlockSpec((bsz, d), lambda j: (0, 0)),
                  pl.BlockSpec((d, tn), lambda j: (0, j)),
                  pl.BlockSpec((1, tn), lambda j: (0, j))],
        out_specs=pl.BlockSpec((bsz, tn), lambda j: (0, j)),
        out_shape=jax.ShapeDtypeStruct((bsz, n), F32),
        compiler_params=_cparams(("arbitrary",)),
        name="adaln",
    )(c, w_ada, b_ada.reshape(1, n))


def _norm_mod_kernel(x_ref, w_ref, shift_ref, scale_ref, o_ref):
    y = _rms(x_ref[...]) * w_ref[...]
    o_ref[...] = (y * (1.0 + scale_ref[...]) + shift_ref[...]).astype(o_ref.dtype)


def _norm_mod(x, w, mod3, shift_idx, scale_idx):
    bsz, seq, d = x.shape
    tm = 512
    return pl.pallas_call(
        _norm_mod_kernel,
        grid=(bsz, seq // tm),
        in_specs=[pl.BlockSpec((None, tm, d), lambda b, i: (b, i, 0)),
                  pl.BlockSpec((1, d), lambda b, i: (0, 0)),
                  pl.BlockSpec((None, 1, d), lambda b, i: (b, 0, shift_idx)),
                  pl.BlockSpec((None, 1, d), lambda b, i: (b, 0, scale_idx))],
        out_specs=pl.BlockSpec((None, tm, d), lambda b, i: (b, i, 0)),
        out_shape=jax.ShapeDtypeStruct((bsz, seq, d), BF16),
        compiler_params=_cparams(("arbitrary", "arbitrary")),
        name="norm_mod",
    )(x, w.reshape(1, d), mod3, mod3)


def _mm_kernel(x_ref, w_ref, o_ref):
    o_ref[...] = jnp.dot(x_ref[...], w_ref[...], preferred_element_type=F32).astype(o_ref.dtype)


def _matmul(x, w, tm, tn, out_dtype, name):
    m, k = x.shape
    n = w.shape[1]
    return pl.pallas_call(
        _mm_kernel,
        grid=(n // tn, m // tm),
        in_specs=[pl.BlockSpec((tm, k), lambda j, i: (i, 0)),
                  pl.BlockSpec((k, tn), lambda j, i: (0, j))],
        out_specs=pl.BlockSpec((tm, tn), lambda j, i: (i, j)),
        out_shape=jax.ShapeDtypeStruct((m, n), out_dtype),
        compiler_params=_cparams(("arbitrary", "arbitrary")),
        name=name,
    )(x, w)


def _prep_kernel(sm_ref, bias_ref, alog_ref, tril_ref, eye_ref, p1_ref, p2_ref, p2t_ref, *, n_dt, n_beta):
    sm = sm_ref[...]
    lane = lax.broadcasted_iota(jnp.int32, sm.shape, 1)
    is_dt = lane < n_dt
    is_beta = (lane >= n_dt) & (lane < n_dt + n_beta)
    sp = _softplus(sm + bias_ref[...])
    rate = -jnp.exp(alog_ref[...]) * sp
    p1_ref[...] = jnp.where(is_beta, _sigmoid(sm), jnp.where(is_dt, sp, rate))
    cum = _dot01_left(tril_ref[...], jnp.where(is_beta, 0.0, rate))
    p2_ref[...] = cum
    p2t_ref[...] = _dot01_tn(cum, eye_ref[...])


def _prep(small, bias, alog, n_dt, n_beta):
    t = small.shape[0]
    ts = SEQ_TILE
    r = jnp.arange(ts)
    tril = ((r[:, None] // CHUNK == r[None, :] // CHUNK) & (r[:, None] >= r[None, :])).astype(BF16)
    eye = (r[:, None] == r[None, :]).astype(BF16)
    return pl.pallas_call(
        functools.partial(_prep_kernel, n_dt=n_dt, n_beta=n_beta),
        grid=(t // ts,),
        in_specs=[pl.BlockSpec((ts, SMALL_W), lambda i: (i, 0)),
                  pl.BlockSpec((1, SMALL_W), lambda i: (0, 0)),
                  pl.BlockSpec((1, SMALL_W), lambda i: (0, 0)),
                  pl.BlockSpec((ts, ts), lambda i: (0, 0)),
                  pl.BlockSpec((ts, ts), lambda i: (0, 0))],
        out_specs=[pl.BlockSpec((ts, SMALL_W), lambda i: (i, 0)),
                   pl.BlockSpec((ts, SMALL_W), lambda i: (i, 0)),
                   pl.BlockSpec((SMALL_W, ts), lambda i: (0, i))],
        out_shape=[jax.ShapeDtypeStruct((t, SMALL_W), F32),
                   jax.ShapeDtypeStruct((t, SMALL_W), F32),
                   jax.ShapeDtypeStruct((SMALL_W, t), F32)],
        compiler_params=_cparams(("arbitrary",)),
        name="head_scalars",
    )(small, bias, alog, tril, eye)


def _ssd_kernel(z_ref, x_ref, b_ref, c_ref, p1_ref, p2_ref, arow_ref, e_ref, causal_ref, headbd_ref,
                cwx_ref, cwb_ref, cwc_ref, cbx_ref, cbb_ref, cbc_ref, dsk_ref, nw_ref, o_ref,
                xs_ref, bs_ref, cs_ref, st_ref):
    rows = x_ref.shape[0]
    gw = x_ref.shape[1]
    n_heads = gw // SSM_HEAD_DIM

    @pl.when(pl.program_id(2) == 0)
    def _():
        xs_ref[0:HALO, :] = jnp.zeros((HALO, gw), F32)
        bs_ref[0:HALO, :] = jnp.zeros((HALO, SSM_D_STATE), F32)
        cs_ref[0:HALO, :] = jnp.zeros((HALO, SSM_D_STATE), F32)
        st_ref[...] = jnp.zeros_like(st_ref)

    xa = _conv_silu(xs_ref, x_ref[...], cwx_ref, cbx_ref[...], rows)
    ba = _conv_silu(bs_ref, b_ref[...], cwb_ref, cbb_ref[...], rows)
    ca = _conv_silu(cs_ref, c_ref[...], cwc_ref, cbc_ref[...], rows)

    e = e_ref[...]
    dt = _dot01_right(p1_ref[...], e)
    a_cum = _dot01_right(p2_ref[...], e)
    xdt = xa * dt
    ea = jnp.exp(a_cum)
    head_bd = headbd_ref[...]

    ys = []
    for ci in range(rows // CHUNK):
        sl = slice(ci * CHUNK, (ci + 1) * CHUNK)
        b_c, c_c = ba[sl], ca[sl]
        ac = a_cum[sl]
        a_last = ac[CHUNK - 1:CHUNK, :]
        xdt_c = xdt[sl]
        a_row = jnp.concatenate([arow_ref[h:h + 1, sl] for h in range(n_heads)], axis=1)
        decay = jnp.exp(jnp.minimum(ac - a_row, 0.0)) * causal_ref[...]
        cb = _bdot_nt(c_c, jnp.concatenate([b_c] * n_heads, axis=0))
        scores = cb * decay
        x_bd = jnp.concatenate([xdt_c.astype(BF16)] * n_heads, axis=0) * head_bd
        y_diag = jnp.dot(scores.astype(BF16), x_bd, preferred_element_type=F32)
        state = st_ref[...]
        y_off = _bdot(c_c, state) * ea[sl]
        xw = xdt_c * jnp.exp(a_last - ac)
        st_ref[...] = state * jnp.exp(a_last) + _bdot_tn(b_c, xw)
        ys.append(y_diag + y_off)
    y = jnp.concatenate(ys, axis=0) + dsk_ref[...] * xa
    y = y * _silu(z_ref[...])
    o_ref[...] = (_rms(y) * nw_ref[...]).astype(o_ref.dtype)


def _ssd(proj, p1, p2, p2t, e_sel, conv_w, conv_b, dsk_e, norm_w, bsz, seq, offs):
    d_in = norm_w.shape[1]
    gw = d_in // SSM_N_GROUPS
    hpg = gw // SSM_HEAD_DIM
    n = SSM_D_STATE
    ts = SEQ_TILE
    nt = seq // ts
    z0 = offs["ssm_z"] // gw
    x0 = offs["ssm_x"] // gw
    b0 = offs["ssm_b"] // n
    c0 = offs["ssm_c"] // n
    cb0 = d_in // n
    cc0 = (d_in + SSM_N_GROUPS * n) // n
    row = lambda b, g, i: b * nt + i
    r = jnp.arange(CHUNK)[:, None]
    l = jnp.arange(gw)[None, :]
    causal = (r >= l % SSM_HEAD_DIM).astype(F32)
    k = jnp.arange(gw)
    head_bd = (k[:, None] // SSM_HEAD_DIM == k[None, :] // SSM_HEAD_DIM).astype(BF16)
    return pl.pallas_call(
        _ssd_kernel,
        grid=(bsz, SSM_N_GROUPS, nt),
        in_specs=[
            pl.BlockSpec((ts, gw), lambda b, g, i: (row(b, g, i), z0 + g)),
            pl.BlockSpec((ts, gw), lambda b, g, i: (row(b, g, i), x0 + g)),
            pl.BlockSpec((ts, n), lambda b, g, i: (row(b, g, i), b0 + g)),
            pl.BlockSpec((ts, n), lambda b, g, i: (row(b, g, i), c0 + g)),
            pl.BlockSpec((ts, SMALL_W), lambda b, g, i: (row(b, g, i), 0)),
            pl.BlockSpec((ts, SMALL_W), lambda b, g, i: (row(b, g, i), 0)),
            pl.BlockSpec((hpg, ts), lambda b, g, i: (g, row(b, g, i))),
            pl.BlockSpec((None, SMALL_W, gw), lambda b, g, i: (g, 0, 0)),
            pl.BlockSpec((CHUNK, gw), lambda b, g, i: (0, 0)),
            pl.BlockSpec((gw, gw), lambda b, g, i: (0, 0)),
            pl.BlockSpec((CONV_K, gw), lambda b, g, i: (0, g)),
            pl.BlockSpec((CONV_K, n), lambda b, g, i: (0, cb0 + g)),
            pl.BlockSpec((CONV_K, n), lambda b, g, i: (0, cc0 + g)),
            pl.BlockSpec((1, gw), lambda b, g, i: (0, g)),
            pl.BlockSpec((1, n), lambda b, g, i: (0, cb0 + g)),
            pl.BlockSpec((1, n), lambda b, g, i: (0, cc0 + g)),
            pl.BlockSpec((1, gw), lambda b, g, i: (0, g)),
            pl.BlockSpec((1, gw), lambda b, g, i: (0, g)),
        ],
        out_specs=pl.BlockSpec((ts, gw), lambda b, g, i: (row(b, g, i), g)),
        out_shape=jax.ShapeDtypeStruct((bsz * seq, d_in), BF16),
        scratch_shapes=[pltpu.VMEM((ts + HALO, gw), F32),
                        pltpu.VMEM((ts + HALO, n), F32),
                        pltpu.VMEM((ts + HALO, n), F32),
                        pltpu.VMEM((n, gw), F32)],
        compiler_params=_cparams(("arbitrary", "arbitrary", "arbitrary")),
        name="ssd_mixer",
    )(proj, proj, proj, proj, p1, p2, p2t, e_sel, causal, head_bd,
      conv_w, conv_w, conv_w, conv_b, conv_b, conv_b, dsk_e, norm_w)


def _gdn_kernel(q_ref, k_ref, v_ref, z_ref, p1_ref, p2_ref, grow_ref, eb_ref, eg_ref,
                incl_ref, eye_ref, nbase_ref, e1_ref, e2_ref,
                cwq_ref, cwk_ref, cwv_ref, nw_ref, o_ref,
                qs_ref, ks_ref, vs_ref, st_ref):
    rows = q_ref.shape[0]
    dk = dv = GDN_HEAD_DIM
    n_qk = q_ref.shape[1] // dk
    n_v = v_ref.shape[1] // dv
    assert rows == 2 * dk

    @pl.when(pl.program_id(2) == 0)
    def _():
        qs_ref[0:HALO, :] = jnp.zeros((HALO, n_qk * dk), F32)
        ks_ref[0:HALO, :] = jnp.zeros((HALO, n_qk * dk), F32)
        vs_ref[0:HALO, :] = jnp.zeros((HALO, n_v * dv), F32)
        st_ref[...] = jnp.zeros_like(st_ref)

    q_all = _conv_silu(qs_ref, q_ref[...], cwq_ref, None, rows)
    k_all = _conv_silu(ks_ref, k_ref[...], cwk_ref, None, rows)
    v_all = _conv_silu(vs_ref, v_ref[...], cwv_ref, None, rows)
    beta = _dot01_right(p1_ref[...], eb_ref[...])
    gam = _dot01_right(p2_ref[...], eg_ref[...])
    e_gam = jnp.exp(gam)
    incl_f = incl_ref[...]
    g_off = (pl.program_id(1) * n_v) % SUBLANES

    qs, ks, kks, qks = [], [], [], []
    for j in range(n_qk):
        js = slice(j * dk, (j + 1) * dk)
        q = q_all[:, js]
        k = k_all[:, js]
        q = q * (lax.rsqrt(jnp.sum(q * q, axis=-1, keepdims=True) + L2_EPS) * (dk ** -0.5))
        k = k * lax.rsqrt(jnp.sum(k * k, axis=-1, keepdims=True) + L2_EPS)
        qs.append(q)
        ks.append(k)
        kks.append(_bdot_nt(k, k))
        qks.append(_bdot_nt(q, k))

    heads = range(n_v)
    eye = eye_ref[...]
    t_inv, sq, e_lo, rhs, qkm, gbs = [], [], [], [], [], []
    for h in heads:
        j = h // GDN_V_PER_QK
        hs = slice(h * dv, (h + 1) * dv)
        gb = gam[:, hs]
        bb = beta[:, hs]
        seg = jnp.concatenate([gb, gb], axis=1) - grow_ref[pl.ds(g_off + h, 1), :]
        dec = jnp.exp(jnp.minimum(seg, 0.0))
        a = kks[j] * dec * jnp.concatenate([bb, bb], axis=1)
        b0 = a * nbase_ref[...]
        sq.append(b0)
        t_inv.append(eye + b0)
        e_lo.append([(a * m_ref[...]).astype(BF16) for m_ref in (e1_ref, e2_ref)])
        rhs.append(jnp.concatenate([v_all[:, hs] * bb, ks[j] * (bb * e_gam[:, hs])], axis=1))
        qkm.append((qks[j] * dec * incl_f).astype(BF16))
        gbs.append(gb)

    n_sq = INV_BASE.bit_length() - 2
    for h in heads:
        sq[h] = _bdot(sq[h], sq[h])
    for lvl in range(n_sq):
        for h in heads:
            if lvl == n_sq - 1:
                t_inv[h] = t_inv[h] + _bdot(t_inv[h], sq[h])
            else:
                res = _bdot(jnp.concatenate([t_inv[h], sq[h]], axis=0), sq[h])
                t_inv[h] = t_inv[h] + res[:rows]
                sq[h] = res[rows:]
    for lvl in range(len(e_lo[0])):
        for h in heads:
            t_inv[h] = t_inv[h] - _bdot(t_inv[h], _bdot(e_lo[h][lvl], t_inv[h]))
    xs = [_bdot(t_inv[h], rhs[h]) for h in heads]

    qd = [qs[h // GDN_V_PER_QK] * e_gam[:, h * dv:(h + 1) * dv] for h in heads]
    qs_parts = [[] for _ in heads]
    vn_parts = [[] for _ in heads]
    for ci in range(rows // CHUNK):
        sl = slice(ci * CHUNK, (ci + 1) * CHUNK)
        for h in heads:
            j = h // GDN_V_PER_QK
            u_c, w_c = xs[h][sl, :dv], xs[h][sl, dv:]
            gb = gbs[h]
            gl = gb[ci * CHUNK + CHUNK - 1:(ci + 1) * CHUNK, :]
            state = st_ref[h]
            wq = _bdot(jnp.concatenate([w_c, qd[h][sl]], axis=0), state)
            v_new = u_c - wq[:CHUNK]
            k_tail = ks[j][sl] * jnp.exp(gl - gb[sl])
            st_ref[h] = state * jnp.exp(gl) + _bdot_tn(k_tail, v_new)
            qs_parts[h].append(wq[CHUNK:])
            vn_parts[h].append(v_new)
    z = z_ref[...]
    for h in heads:
        hs = slice(h * dv, (h + 1) * dv)
        o = jnp.concatenate(qs_parts[h], axis=0) + jnp.dot(
            qkm[h], jnp.concatenate(vn_parts[h], axis=0).astype(BF16), preferred_element_type=F32)
        o = _rms(o) * nw_ref[...] * _silu(z[:, hs])
        o_ref[:, hs] = o.astype(o_ref.dtype)


def _gdn(proj, p1, p2, p2t, e_beta, e_gam, conv_w, norm_w, bsz, seq, offs, n_qk, g_row0):
    dk = GDN_HEAD_DIM
    qb = GDN_QK_PER_STEP
    vb = qb * GDN_V_PER_QK
    qw = qb * dk
    vw = vb * GDN_HEAD_DIM
    ts = SEQ_TILE
    nt = seq // ts
    q0 = offs["gdn_q"] // qw
    k0 = offs["gdn_k"] // qw
    v0 = offs["gdn_v"] // vw
    z0 = offs["gdn_z"] // vw
    ck0 = (n_qk * dk) // qw
    cv0 = (2 * n_qk * dk) // vw
    assert g_row0 % SUBLANES == 0 and (SUBLANES % vb == 0 or vb % SUBLANES == 0)
    row = lambda b, g, i: b * nt + i
    r = jnp.arange(ts)
    same = r[:, None] // CHUNK == r[None, :] // CHUNK
    lower = r[:, None] > r[None, :]
    incl = (same & (r[:, None] >= r[None, :])).astype(F32)
    eye = (r[:, None] == r[None, :]).astype(F32)
    blk = lambda size: r[:, None] // size == r[None, :] // size
    nbase = -(blk(INV_BASE) & lower).astype(F32)
    e_masks = []
    size = INV_BASE
    while size < CHUNK:
        e_masks.append((blk(2 * size) & ~blk(size) & lower).astype(F32))
        size *= 2
    assert len(e_masks) == 2
    return pl.pallas_call(
        _gdn_kernel,
        grid=(bsz, n_qk // qb, nt),
        in_specs=[
            pl.BlockSpec((ts, qw), lambda b, g, i: (row(b, g, i), q0 + g)),
            pl.BlockSpec((ts, qw), lambda b, g, i: (row(b, g, i), k0 + g)),
            pl.BlockSpec((ts, vw), lambda b, g, i: (row(b, g, i), v0 + g)),
            pl.BlockSpec((ts, vw), lambda b, g, i: (row(b, g, i), z0 + g)),
            pl.BlockSpec((ts, SMALL_W), lambda b, g, i: (row(b, g, i), 0)),
            pl.BlockSpec((ts, SMALL_W), lambda b, g, i: (row(b, g, i), 0)),
            pl.BlockSpec((max(vb, SUBLANES), ts),
                         lambda b, g, i: ((g_row0 + g * vb) // max(vb, SUBLANES), row(b, g, i))),
            pl.BlockSpec((None, SMALL_W, vw), lambda b, g, i: (g, 0, 0)),
            pl.BlockSpec((None, SMALL_W, vw), lambda b, g, i: (g, 0, 0)),
            pl.BlockSpec((ts, ts), lambda b, g, i: (0, 0)),
            pl.BlockSpec((ts, ts), lambda b, g, i: (0, 0)),
            pl.BlockSpec((ts, ts), lambda b, g, i: (0, 0)),
            pl.BlockSpec((ts, ts), lambda b, g, i: (0, 0)),
            pl.BlockSpec((ts, ts), lambda b, g, i: (0, 0)),
            pl.BlockSpec((CONV_K, qw), lambda b, g, i: (0, g)),
            pl.BlockSpec((CONV_K, qw), lambda b, g, i: (0, ck0 + g)),
            pl.BlockSpec((CONV_K, vw), lambda b, g, i: (0, cv0 + g)),
            pl.BlockSpec((1, dk), lambda b, g, i: (0, 0)),
        ],
        out_specs=pl.BlockSpec((ts, vw), lambda b, g, i: (row(b, g, i), g)),
        out_shape=jax.ShapeDtypeStruct((bsz * seq, n_qk * GDN_V_PER_QK * GDN_HEAD_DIM), BF16),
        scratch_shapes=[pltpu.VMEM((ts + HALO, qw), F32),
                        pltpu.VMEM((ts + HALO, qw), F32),
                        pltpu.VMEM((ts + HALO, vw), F32),
                        pltpu.VMEM((vb, dk, GDN_HEAD_DIM), F32)],
        compiler_params=_cparams(("arbitrary", "arbitrary", "arbitrary")),
        name="gdn_mixer",
    )(proj, proj, proj, proj, p1, p2, p2t, e_beta, e_gam, incl, eye, nbase, e_masks[0], e_masks[1],
      conv_w, conv_w, conv_w, norm_w)


def _merge_kernel(ys_ref, yg_ref, gs_ref, gg_ref, ws_ref, wg_ref, o_ref):
    a = jnp.dot(ys_ref[...], ws_ref[...], preferred_element_type=F32)
    b = jnp.dot(yg_ref[...], wg_ref[...], preferred_element_type=F32)
    o_ref[...] = (_sigmoid(gs_ref[...]) * a + _sigmoid(gg_ref[...]) * b).astype(o_ref.dtype)


def _merge(y_ssm, y_gdn, proj, w_ssm, w_gdn, offs):
    m, k = y_ssm.shape
    n = w_ssm.shape[1]
    tm, tn = 512, 512
    gs0 = offs["gate_ssm"] // tn
    gg0 = offs["gate_gdn"] // tn
    return pl.pallas_call(
        _merge_kernel,
        grid=(n // tn, m // tm),
        in_specs=[pl.BlockSpec((tm, k), lambda j, i: (i, 0)),
                  pl.BlockSpec((tm, k), lambda j, i: (i, 0)),
                  pl.BlockSpec((tm, tn), lambda j, i: (i, gs0 + j)),
                  pl.BlockSpec((tm, tn), lambda j, i: (i, gg0 + j)),
                  pl.BlockSpec((k, tn), lambda j, i: (0, j)),
                  pl.BlockSpec((k, tn), lambda j, i: (0, j))],
        out_specs=pl.BlockSpec((tm, tn), lambda j, i: (i, j)),
        out_shape=jax.ShapeDtypeStruct((m, n), BF16),
        compiler_params=_cparams(("arbitrary", "arbitrary")),
        name="merge_proj",
    )(y_ssm, y_gdn, proj, proj, w_ssm, w_gdn)


def _mixout_kernel(m_ref, wo_ref, x_ref, gate_ref, nw_ref, shift_ref, scale_ref, x1_ref, h2_ref):
    mix = jnp.dot(m_ref[...], wo_ref[...], preferred_element_type=F32)
    x1 = x_ref[...] + gate_ref[...] * mix
    x1_ref[...] = x1
    y = _rms(x1) * nw_ref[...]
    h2_ref[...] = (y * (1.0 + scale_ref[...]) + shift_ref[...]).astype(h2_ref.dtype)


def _mixout(merged, w_o, x, mod3, norm_w, gate_idx, shift_idx, scale_idx):
    bsz, seq, d = x.shape
    tm = 512
    nt = seq // tm
    return pl.pallas_call(
        _mixout_kernel,
        grid=(bsz, nt),
        in_specs=[pl.BlockSpec((tm, d), lambda b, i: (b * nt + i, 0)),
                  pl.BlockSpec((d, d), lambda b, i: (0, 0)),
                  pl.BlockSpec((None, tm, d), lambda b, i: (b, i, 0)),
                  pl.BlockSpec((None, 1, d), lambda b, i: (b, 0, gate_idx)),
                  pl.BlockSpec((1, d), lambda b, i: (0, 0)),
                  pl.BlockSpec((None, 1, d), lambda b, i: (b, 0, shift_idx)),
                  pl.BlockSpec((None, 1, d), lambda b, i: (b, 0, scale_idx))],
        out_specs=[pl.BlockSpec((None, tm, d), lambda b, i: (b, i, 0)),
                   pl.BlockSpec((None, tm, d), lambda b, i: (b, i, 0))],
        out_shape=[jax.ShapeDtypeStruct((bsz, seq, d), F32),
                   jax.ShapeDtypeStruct((bsz, seq, d), BF16)],
        compiler_params=_cparams(("arbitrary", "arbitrary")),
        name="mix_out",
    )(merged, w_o, x, mod3, norm_w.reshape(1, d), mod3, mod3)


def _gateup_kernel(h_ref, wg_ref, wu_ref, o_ref):
    h = h_ref[...]
    g = jnp.dot(h, wg_ref[...], preferred_element_type=F32)
    u = jnp.dot(h, wu_ref[...], preferred_element_type=F32)
    o_ref[...] = (_silu(g) * u).astype(o_ref.dtype)


def _gateup(h, w_gate_up):
    m, k = h.shape
    hid = w_gate_up.shape[1] // 2
    tm, tn = 1024, 512
    up0 = hid // tn
    return pl.pallas_call(
        _gateup_kernel,
        grid=(hid // tn, m // tm),
        in_specs=[pl.BlockSpec((tm, k), lambda j, i: (i, 0)),
                  pl.BlockSpec((k, tn), lambda j, i: (0, j)),
                  pl.BlockSpec((k, tn), lambda j, i: (0, up0 + j))],
        out_specs=pl.BlockSpec((tm, tn), lambda j, i: (i, j)),
        out_shape=jax.ShapeDtypeStruct((m, hid), BF16),
        compiler_params=_cparams(("arbitrary", "arbitrary")),
        name="ffn_gate_up",
    )(h, w_gate_up, w_gate_up)


def _down_kernel(a_ref, wd_ref, x1_ref, gate_ref, fw_ref, o_ref, *, final):
    ffn = jnp.dot(a_ref[...], wd_ref[...], preferred_element_type=F32)
    x2 = x1_ref[...] + gate_ref[...] * ffn
    if final:
        x2 = _rms(x2) * fw_ref[...]
    o_ref[...] = x2


def _down(act, w_down, x1, mod3, final_w, gate_idx, final):
    bsz, seq, d = x1.shape
    hid = act.shape[1]
    tm = 256
    nt = seq // tm
    return pl.pallas_call(
        functools.partial(_down_kernel, final=final),
        grid=(bsz, nt),
        in_specs=[pl.BlockSpec((tm, hid), lambda b, i: (b * nt + i, 0)),
                  pl.BlockSpec((hid, d), lambda b, i: (0, 0), pipeline_mode=pl.Buffered(1)),
                  pl.BlockSpec((None, tm, d), lambda b, i: (b, i, 0)),
                  pl.BlockSpec((None, 1, d), lambda b, i: (b, 0, gate_idx)),
                  pl.BlockSpec((1, d), lambda b, i: (0, 0))],
        out_specs=pl.BlockSpec((None, tm, d), lambda b, i: (b, i, 0)),
        out_shape=jax.ShapeDtypeStruct((bsz, seq, d), F32),
        compiler_params=_cparams(("arbitrary", "arbitrary")),
        name="ffn_down",
    )(act, w_down, x1, mod3, final_w.reshape(1, d))


def _selector(n_groups, src0, per_group, width, rep):
    k = jnp.arange(SMALL_W)[None, :, None]
    g = jnp.arange(n_groups)[:, None, None]
    c = jnp.arange(width)[None, None, :]
    return (k == src0 + g * per_group + c // rep).astype(BF16)


def kernel(x, c, w_ada, b_ada, norm_mix_w, w_in, ssm_conv_w, ssm_conv_b, ssm_dt_bias, ssm_a_log, ssm_d_skip, ssm_norm_w, gdn_conv_w, gdn_a_log, gdn_dt_bias, gdn_norm_w, w_ssm_proj, w_gdn_proj, w_o, norm_ffn_w, w_gate_up, w_down, final_norm_w):
    bsz, seq, d = x.shape
    depth = w_ada.shape[0]
    d_in = ssm_norm_w.shape[1]
    n_ssm_heads = ssm_dt_bias.shape[1]
    gn = SSM_N_GROUPS * SSM_D_STATE
    n_v = gdn_a_log.shape[1]
    n_qk = n_v // GDN_V_PER_QK
    qk_dim = n_qk * GDN_HEAD_DIM
    v_dim = n_v * GDN_HEAD_DIM
    assert n_ssm_heads + 2 * n_v == SMALL_W and seq % SEQ_TILE == 0

    sizes = (d_in, d_in + 2 * gn, n_ssm_heads, 2 * qk_dim + v_dim, v_dim, n_v, n_v, d, d)
    starts = [0]
    for s in sizes:
        starts.append(starts[-1] + s)
    (s_z, s_xbc, s_dt, s_qkv, s_gz, s_beta, s_a, s_gs, s_gg) = starts[:-1]
    offs = {"ssm_z": 0, "ssm_x": d_in, "ssm_b": 2 * d_in, "ssm_c": 2 * d_in + gn}
    o_qkv = 2 * d_in + 2 * gn
    offs.update({"gdn_q": o_qkv, "gdn_k": o_qkv + qk_dim, "gdn_v": o_qkv + 2 * qk_dim})
    offs["gdn_z"] = o_qkv + 2 * qk_dim + v_dim
    offs["gate_ssm"] = offs["gdn_z"] + v_dim
    offs["gate_gdn"] = offs["gate_ssm"] + d

    vb = GDN_QK_PER_STEP * GDN_V_PER_QK
    e_dt = _selector(SSM_N_GROUPS, 0, n_ssm_heads // SSM_N_GROUPS, d_in // SSM_N_GROUPS, SSM_HEAD_DIM)
    e_beta = _selector(n_v // vb, n_ssm_heads, vb, vb * GDN_HEAD_DIM, GDN_HEAD_DIM)
    e_gam = _selector(n_v // vb, n_ssm_heads + n_v, vb, vb * GDN_HEAD_DIM, GDN_HEAD_DIM)

    out = x
    for layer in range(depth):
        wl = w_in[layer]
        w_main = jnp.concatenate([wl[:, s_z:s_dt], wl[:, s_qkv:s_beta], wl[:, s_gs:]], axis=1).astype(BF16)
        w_small = jnp.concatenate([wl[:, s_dt:s_qkv], wl[:, s_beta:s_gs]], axis=1).astype(BF16)

        mod = _adaln(c, w_ada[layer], b_ada[layer])
        mod3 = mod.reshape(bsz, 1, 6 * d)
        h = _norm_mod(out, norm_mix_w[layer], mod3, 0, 1).reshape(bsz * seq, d)
        proj = _matmul(h, w_main, 1024, 1024, F32, "in_proj")
        small = _matmul(h, w_small, 1024, SMALL_W, F32, "in_proj_small")

        zeros_b = jnp.zeros((n_v,), F32)
        bias = jnp.concatenate([ssm_dt_bias[layer], zeros_b, gdn_dt_bias[layer]])[None, :]
        alog = jnp.concatenate([ssm_a_log[layer], zeros_b, gdn_a_log[layer]])[None, :]
        p1, p2, p2t = _prep(small, bias, alog, n_ssm_heads, n_v)

        y_ssm = _ssd(proj, p1, p2, p2t, e_dt, ssm_conv_w[layer], ssm_conv_b[layer][None, :],
                     jnp.repeat(ssm_d_skip[layer], SSM_HEAD_DIM)[None, :], ssm_norm_w[layer][None, :],
                     bsz, seq, offs)
        y_gdn = _gdn(proj, p1, p2, p2t, e_beta, e_gam, gdn_conv_w[layer], gdn_norm_w[layer][None, :],
                     bsz, seq, offs, n_qk, n_ssm_heads + n_v)

        merged = _merge(y_ssm, y_gdn, proj, w_ssm_proj[layer].astype(BF16), w_gdn_proj[layer].astype(BF16), offs)
        x1, h2 = _mixout(merged, w_o[layer].astype(BF16), out, mod3, norm_ffn_w[layer], 2, 3, 4)
        act = _gateup(h2.reshape(bsz * seq, d), w_gate_up[layer].astype(BF16))
        out = _down(act, w_down[layer].astype(BF16), x1, mod3, final_norm_w, 5, layer == depth - 1)
    return out
```

```python
import functools

import jax
import jax.numpy as jnp
from jax import lax
from jax.experimental import pallas as pl
from jax.experimental.pallas import tpu as pltpu

F32 = jnp.float32
BF16 = jnp.bfloat16

NORM_EPS = 1e-6
L2_EPS = 1e-6
CONV_K = 4
SUBLANES = 8
HALO = 8
CHUNK = 64
INV_BASE = 16
SEQ_TILE = 256
SSD_TILE = 1024
SSM_HEAD_DIM = 64
SSM_N_GROUPS = 8
SSM_D_STATE = 128
GDN_HEAD_DIM = 128
GDN_V_PER_QK = 2
GDN_QK_PER_STEP = 4
SMALL_W = 128
INPROJ_ROWS = 256
VMEM_LIMIT = 56 * 1024 * 1024


def _cparams(sem):
    return pltpu.CompilerParams(dimension_semantics=sem, vmem_limit_bytes=VMEM_LIMIT)


def _bdot(a, b):
    return jnp.dot(a.astype(BF16), b.astype(BF16), preferred_element_type=F32)


def _bdot_nt(a, b):
    return lax.dot_general(a.astype(BF16), b.astype(BF16), (((1,), (1,)), ((), ())),
                           preferred_element_type=F32)


def _bdot_tn(a, b):
    return lax.dot_general(a.astype(BF16), b.astype(BF16), (((0,), (0,)), ((), ())),
                           preferred_element_type=F32)


def _split3(x):
    hi = x.astype(BF16)
    r1 = x - hi.astype(F32)
    mid = r1.astype(BF16)
    lo = (r1 - mid.astype(F32)).astype(BF16)
    return hi, mid, lo


def _dot01_left(m01, x):
    hi, mid, lo = _split3(x)
    d = lambda p: jnp.dot(m01, p, preferred_element_type=F32)
    return (d(lo) + d(mid)) + d(hi)


def _dot01_right(x, m01):
    hi, mid, lo = _split3(x)
    d = lambda p: jnp.dot(p, m01, preferred_element_type=F32)
    return (d(lo) + d(mid)) + d(hi)


def _dot01_tn(x, m01):
    hi, mid, lo = _split3(x)
    d = lambda p: lax.dot_general(p, m01, (((0,), (0,)), ((), ())), preferred_element_type=F32)
    return (d(lo) + d(mid)) + d(hi)


def _sigmoid(x):
    return 1.0 / (1.0 + jnp.exp(-x))


def _silu(x):
    return x * _sigmoid(x)


def _softplus(x):
    return jnp.maximum(x, 0.0) + jnp.log(1.0 + jnp.exp(-jnp.abs(x)))


def _rms(x, eps=NORM_EPS):
    return x * lax.rsqrt(jnp.mean(x * x, axis=-1, keepdims=True) + eps)


def _conv_silu(stage_ref, cur, w_ref, bias, rows):
    stage_ref[HALO:HALO + rows, :] = cur
    acc = None
    for k in reversed(range(CONV_K)):
        start = HALO - (CONV_K - 1) + k
        term = stage_ref[start:start + rows, :] * w_ref[k:k + 1, :]
        acc = term if acc is None else acc + term
    if bias is not None:
        acc = acc + bias
    stage_ref[0:HALO, :] = cur[rows - HALO:rows, :]
    return _silu(acc)


def _ada_kernel(c_ref, w_ref, b_ref, o_ref):
    c_act = _silu(c_ref[...])
    o_ref[...] = _bdot(c_act, w_ref[...]) + b_ref[...]


def _adaln(c, w_ada, b_ada):
    bsz, d = c.shape
    n = w_ada.shape[1]
    tn = 1024
    return pl.pallas_call(
        _ada_kernel,
        grid=(n // tn,),
        in_specs=[pl.BlockSpec((bsz, d), lambda j: (0, 0)),
                  pl.BlockSpec((d, tn), lambda j: (0, j)),
                  pl.BlockSpec((1, tn), lambda j: (0, j))],
        out_specs=pl.BlockSpec((bsz, tn), lambda j: (0, j)),
        out_shape=jax.ShapeDtypeStruct((bsz, n), F32),
        compiler_params=_cparams(("arbitrary",)),
        name="adaln",
    )(c, w_ada, b_ada.reshape(1, n))


def _norm_mod_kernel(x_ref, w_ref, shift_ref, scale_ref, o_ref):
    y = _rms(x_ref[...]) * w_ref[...]
    o_ref[...] = (y * (1.0 + scale_ref[...]) + shift_ref[...]).astype(o_ref.dtype)


def _norm_mod(x, w, mod3, shift_idx, scale_idx):
    bsz, seq, d = x.shape
    tm = 512
    return pl.pallas_call(
        _norm_mod_kernel,
        grid=(bsz, seq // tm),
        in_specs=[pl.BlockSpec((None, tm, d), lambda b, i: (b, i, 0)),
                  pl.BlockSpec((1, d), lambda b, i: (0, 0)),
                  pl.BlockSpec((None, 1, d), lambda b, i: (b, 0, shift_idx)),
                  pl.BlockSpec((None, 1, d), lambda b, i: (b, 0, scale_idx))],
        out_specs=pl.BlockSpec((None, tm, d), lambda b, i: (b, i, 0)),
        out_shape=jax.ShapeDtypeStruct((bsz, seq, d), BF16),
        compiler_params=_cparams(("arbitrary", "arbitrary")),
        name="norm_mod",
    )(x, w.reshape(1, d), mod3, mod3)


def _mm_kernel(x_ref, w_ref, o_ref):
    o_ref[...] = jnp.dot(x_ref[...], w_ref[...], preferred_element_type=F32).astype(o_ref.dtype)


def _matmul(x, w, tm, tn, out_dtype, name):
    m, k = x.shape
    n = w.shape[1]
    return pl.pallas_call(
        _mm_kernel,
        grid=(n // tn, m // tm),
        in_specs=[pl.BlockSpec((tm, k), lambda j, i: (i, 0)),
                  pl.BlockSpec((k, tn), lambda j, i: (0, j))],
        out_specs=pl.BlockSpec((tm, tn), lambda j, i: (i, j)),
        out_shape=jax.ShapeDtypeStruct((m, n), out_dtype),
        compiler_params=_cparams(("arbitrary", "arbitrary")),
        name=name,
    )(x, w)


def _inproj_kernel(x_ref, wa_ref, wb_ref, o_ref, wbf_ref, *, n_shift1, n_shift2, shift1, shift2, silu_tiles, n_sig0):
    n = pl.program_id(0)
    tm, tn = o_ref.shape
    kdim = x_ref.shape[1]

    @pl.when(pl.program_id(1) == 0)
    def _():
        def repack(shift):
            for r0 in range(0, kdim, INPROJ_ROWS):
                rs = slice(r0, r0 + INPROJ_ROWS)
                if shift == 0:
                    w = wa_ref[rs, :]
                else:
                    w = jnp.concatenate([wa_ref[rs, shift:], wb_ref[rs, :shift]], axis=1)
                wbf_ref[rs, :] = w.astype(BF16)

        @pl.when(n < n_shift1)
        def _():
            repack(0)

        @pl.when((n >= n_shift1) & (n < n_shift2))
        def _():
            repack(shift1)

        @pl.when(n >= n_shift2)
        def _():
            repack(shift2)

    def project(act):
        for r0 in range(0, tm, INPROJ_ROWS):
            rs = slice(r0, r0 + INPROJ_ROWS)
            acc = jnp.dot(x_ref[rs, :], wbf_ref[...], preferred_element_type=F32)
            o_ref[rs, :] = acc if act is None else act(acc)

    is_silu = functools.reduce(jnp.logical_or, [(n >= lo) & (n < hi) for lo, hi in silu_tiles])
    is_sig = n >= n_sig0

    @pl.when(is_silu)
    def _():
        project(_silu)

    @pl.when(is_sig)
    def _():
        project(_sigmoid)

    @pl.when(jnp.logical_not(is_silu | is_sig))
    def _():
        project(None)


def _inproj(h, w_in, layer, offs, n_main, shift1, shift2):
    m, k = h.shape
    tm, tn = 1024, 1024
    segs = (offs["ssm_z"], offs["ssm_x"], offs["gdn_z"], offs["gate_ssm"], n_main, offs["gdn_q"])
    assert all(s % tn == 0 for s in segs) and shift2 == SMALL_W and 0 < shift1 < SMALL_W
    silu_tiles = ((offs["ssm_z"] // tn, offs["ssm_x"] // tn), (offs["gdn_z"] // tn, offs["gate_ssm"] // tn))
    return pl.pallas_call(
        functools.partial(_inproj_kernel, n_shift1=offs["gdn_q"] // tn, n_shift2=offs["gate_ssm"] // tn,
                          shift1=shift1, shift2=shift2, silu_tiles=silu_tiles, n_sig0=offs["gate_ssm"] // tn),
        grid=(n_main // tn, m // tm),
        in_specs=[pl.BlockSpec((tm, k), lambda j, i: (i, 0)),
                  pl.BlockSpec((None, k, tn), lambda j, i: (layer, 0, j)),
                  pl.BlockSpec((None, k, SMALL_W), lambda j, i: (layer, 0, (j + 1) * (tn // SMALL_W)))],
        out_specs=pl.BlockSpec((tm, tn), lambda j, i: (i, j)),
        out_shape=jax.ShapeDtypeStruct((m, n_main), F32),
        scratch_shapes=[pltpu.VMEM((k, tn), BF16)],
        compiler_params=_cparams(("arbitrary", "arbitrary")),
        name="in_proj",
    )(h, w_in, w_in)


def _prep_kernel(sm_ref, bias_ref, alog_ref, tril_ref, eye_ref, p1_ref, p2_ref, p2t_ref, *, n_dt, n_beta):
    sm = sm_ref[...]
    lane = lax.broadcasted_iota(jnp.int32, sm.shape, 1)
    is_dt = lane < n_dt
    is_beta = (lane >= n_dt) & (lane < n_dt + n_beta)
    sp = _softplus(sm + bias_ref[...])
    rate = -jnp.exp(alog_ref[...]) * sp
    p1_ref[...] = jnp.where(is_beta, _sigmoid(sm), jnp.where(is_dt, sp, rate))
    cum = _dot01_left(tril_ref[...], jnp.where(is_beta, 0.0, rate))
    p2_ref[...] = cum
    p2t_ref[...] = _dot01_tn(cum, eye_ref[...])


def _prep(small, bias, alog, n_dt, n_beta):
    t = small.shape[0]
    ts = SEQ_TILE
    r = jnp.arange(ts)
    tril = ((r[:, None] // CHUNK == r[None, :] // CHUNK) & (r[:, None] >= r[None, :])).astype(BF16)
    eye = (r[:, None] == r[None, :]).astype(BF16)
    return pl.pallas_call(
        functools.partial(_prep_kernel, n_dt=n_dt, n_beta=n_beta),
        grid=(t // ts,),
        in_specs=[pl.BlockSpec((ts, SMALL_W), lambda i: (i, 0)),
                  pl.BlockSpec((1, SMALL_W), lambda i: (0, 0)),
                  pl.BlockSpec((1, SMALL_W), lambda i: (0, 0)),
                  pl.BlockSpec((ts, ts), lambda i: (0, 0)),
                  pl.BlockSpec((ts, ts), lambda i: (0, 0))],
        out_specs=[pl.BlockSpec((ts, SMALL_W), lambda i: (i, 0)),
                   pl.BlockSpec((ts, SMALL_W), lambda i: (i, 0)),
                   pl.BlockSpec((SMALL_W, ts), lambda i: (0, i))],
        out_shape=[jax.ShapeDtypeStruct((t, SMALL_W), F32),
                   jax.ShapeDtypeStruct((t, SMALL_W), F32),
                   jax.ShapeDtypeStruct((SMALL_W, t), F32)],
        compiler_params=_cparams(("arbitrary",)),
        name="head_scalars",
    )(small, bias, alog, tril, eye)


def _ssd_kernel(z_ref, x_ref, b_ref, c_ref, p1_ref, p2_ref, arow_ref, e_ref, causal_ref, headbd_ref,
                xpar_ref, bpar_ref, cpar_ref, o_ref,
                xs_ref, bs_ref, cs_ref, st_ref):
    rows = x_ref.shape[0]
    gw = x_ref.shape[1]
    n_heads = gw // SSM_HEAD_DIM
    bias_row = slice(CONV_K, CONV_K + 1)
    dsk_row = slice(CONV_K + 1, CONV_K + 2)
    nw_row = slice(CONV_K + 2, CONV_K + 3)

    @pl.when(pl.program_id(2) == 0)
    def _():
        xs_ref[0:HALO, :] = jnp.zeros((HALO, gw), F32)
        bs_ref[0:HALO, :] = jnp.zeros((HALO, SSM_D_STATE), F32)
        cs_ref[0:HALO, :] = jnp.zeros((HALO, SSM_D_STATE), F32)
        st_ref[...] = jnp.zeros_like(st_ref)

    xa = _conv_silu(xs_ref, x_ref[...], xpar_ref, xpar_ref[bias_row, :], rows)
    ba = _conv_silu(bs_ref, b_ref[...], bpar_ref, bpar_ref[bias_row, :], rows)
    ca = _conv_silu(cs_ref, c_ref[...], cpar_ref, cpar_ref[bias_row, :], rows)

    e = e_ref[...]
    dt = _dot01_right(p1_ref[...], e)
    a_cum = _dot01_right(p2_ref[...], e)
    xdt = xa * dt
    ea = jnp.exp(a_cum)
    head_bd = headbd_ref[...]

    ys = []
    for ci in range(rows // CHUNK):
        sl = slice(ci * CHUNK, (ci + 1) * CHUNK)
        b_c, c_c = ba[sl], ca[sl]
        ac = a_cum[sl]
        a_last = ac[CHUNK - 1:CHUNK, :]
        xdt_c = xdt[sl]
        a_row = jnp.concatenate([arow_ref[h:h + 1, sl] for h in range(n_heads)], axis=1)
        decay = jnp.exp(jnp.minimum(ac - a_row, 0.0)) * causal_ref[...]
        cb = _bdot_nt(c_c, jnp.concatenate([b_c] * n_heads, axis=0))
        scores = cb * decay
        x_bd = jnp.concatenate([xdt_c.astype(BF16)] * n_heads, axis=0) * head_bd
        y_diag = jnp.dot(scores.astype(BF16), x_bd, preferred_element_type=F32)
        state = st_ref[...]
        y_off = _bdot(c_c, state) * ea[sl]
        xw = xdt_c * jnp.exp(a_last - ac)
        st_ref[...] = state * jnp.exp(a_last) + _bdot_tn(b_c, xw)
        ys.append(y_diag + y_off)
    y = jnp.concatenate(ys, axis=0) + xpar_ref[dsk_row, :] * xa
    y = y * z_ref[...]
    o_ref[...] = (_rms(y) * xpar_ref[nw_row, :]).astype(o_ref.dtype)


def _ssd(proj, p1, p2, p2t, e_sel, conv_w, conv_b, d_skip, norm_w, bsz, seq, offs):
    d_in = norm_w.shape[0]
    gw = d_in // SSM_N_GROUPS
    hpg = gw // SSM_HEAD_DIM
    n = SSM_D_STATE
    ts = SSD_TILE
    nt = seq // ts
    zrow = jnp.zeros_like(conv_b)
    xtra = lambda v: jnp.concatenate([v, jnp.zeros((conv_b.shape[0] - d_in,), F32)])
    par = jnp.concatenate([conv_w, conv_b[None, :], xtra(jnp.repeat(d_skip, SSM_HEAD_DIM))[None, :],
                           xtra(norm_w)[None, :], zrow[None, :]], axis=0)
    assert par.shape[0] == SUBLANES
    z0 = offs["ssm_z"] // gw
    x0 = offs["ssm_x"] // gw
    b0 = offs["ssm_b"] // n
    c0 = offs["ssm_c"] // n
    cb0 = d_in // n
    cc0 = (d_in + SSM_N_GROUPS * n) // n
    row = lambda b, g, i: b * nt + i
    r = jnp.arange(CHUNK)[:, None]
    l = jnp.arange(gw)[None, :]
    causal = (r >= l % SSM_HEAD_DIM).astype(F32)
    k = jnp.arange(gw)
    head_bd = (k[:, None] // SSM_HEAD_DIM == k[None, :] // SSM_HEAD_DIM).astype(BF16)
    return pl.pallas_call(
        _ssd_kernel,
        grid=(bsz, SSM_N_GROUPS, nt),
        in_specs=[
            pl.BlockSpec((ts, gw), lambda b, g, i: (row(b, g, i), z0 + g)),
            pl.BlockSpec((ts, gw), lambda b, g, i: (row(b, g, i), x0 + g)),
            pl.BlockSpec((ts, n), lambda b, g, i: (row(b, g, i), b0 + g)),
            pl.BlockSpec((ts, n), lambda b, g, i: (row(b, g, i), c0 + g)),
            pl.BlockSpec((ts, SMALL_W), lambda b, g, i: (row(b, g, i), 0)),
            pl.BlockSpec((ts, SMALL_W), lambda b, g, i: (row(b, g, i), 0)),
            pl.BlockSpec((hpg, ts), lambda b, g, i: (g, row(b, g, i))),
            pl.BlockSpec((None, SMALL_W, gw), lambda b, g, i: (g, 0, 0)),
            pl.BlockSpec((CHUNK, gw), lambda b, g, i: (0, 0)),
            pl.BlockSpec((gw, gw), lambda b, g, i: (0, 0)),
            pl.BlockSpec((SUBLANES, gw), lambda b, g, i: (0, g)),
            pl.BlockSpec((SUBLANES, n), lambda b, g, i: (0, cb0 + g)),
            pl.BlockSpec((SUBLANES, n), lambda b, g, i: (0, cc0 + g)),
        ],
        out_specs=pl.BlockSpec((ts, gw), lambda b, g, i: (row(b, g, i), g)),
        out_shape=jax.ShapeDtypeStruct((bsz * seq, d_in), BF16),
        scratch_shapes=[pltpu.VMEM((ts + HALO, gw), F32),
                        pltpu.VMEM((ts + HALO, n), F32),
                        pltpu.VMEM((ts + HALO, n), F32),
                        pltpu.VMEM((n, gw), F32)],
        compiler_params=_cparams(("arbitrary", "arbitrary", "arbitrary")),
        name="ssd_mixer",
    )(proj, proj, proj, proj, p1, p2, p2t, e_sel, causal, head_bd, par, par, par)


def _gdn_kernel(q_ref, k_ref, v_ref, z_ref, p1_ref, p2_ref, grow_ref, eb_ref, eg_ref,
                incl_ref, eye_ref, nbase_ref, e1_ref, e2_ref,
                cwq_ref, cwk_ref, cwv_ref, nw_ref, o_ref,
                qs_ref, ks_ref, vs_ref, st_ref):
    rows = q_ref.shape[0]
    dk = dv = GDN_HEAD_DIM
    n_qk = q_ref.shape[1] // dk
    n_v = v_ref.shape[1] // dv
    assert rows == 2 * dk

    @pl.when(pl.program_id(2) == 0)
    def _():
        qs_ref[0:HALO, :] = jnp.zeros((HALO, n_qk * dk), F32)
        ks_ref[0:HALO, :] = jnp.zeros((HALO, n_qk * dk), F32)
        vs_ref[0:HALO, :] = jnp.zeros((HALO, n_v * dv), F32)
        st_ref[...] = jnp.zeros_like(st_ref)

    q_all = _conv_silu(qs_ref, q_ref[...], cwq_ref, None, rows)
    k_all = _conv_silu(ks_ref, k_ref[...], cwk_ref, None, rows)
    v_all = _conv_silu(vs_ref, v_ref[...], cwv_ref, None, rows)
    beta = _dot01_right(p1_ref[...], eb_ref[...])
    gam = _dot01_right(p2_ref[...], eg_ref[...])
    e_gam = jnp.exp(gam)
    incl_bf = incl_ref[...]
    g_off = (pl.program_id(1) * n_v) % SUBLANES

    qs, ks, kks, qks = [], [], [], []
    for j in range(n_qk):
        js = slice(j * dk, (j + 1) * dk)
        q = q_all[:, js]
        k = k_all[:, js]
        q = q * (lax.rsqrt(jnp.sum(q * q, axis=-1, keepdims=True) + L2_EPS) * (dk ** -0.5))
        k = k * lax.rsqrt(jnp.sum(k * k, axis=-1, keepdims=True) + L2_EPS)
        qs.append(q)
        ks.append(k)
        kks.append(_bdot_nt(k, k))
        qks.append(_bdot_nt(q, k))

    heads = range(n_v)
    eye = eye_ref[...]
    t_inv, sq, e_lo, rhs, qkm, gbs = [], [], [], [], [], []
    for h in heads:
        j = h // GDN_V_PER_QK
        hs = slice(h * dv, (h + 1) * dv)
        gb = gam[:, hs]
        bb = beta[:, hs]
        seg = jnp.concatenate([gb, gb], axis=1) - grow_ref[pl.ds(g_off + h, 1), :]
        dec = jnp.exp(jnp.minimum(seg, 0.0))
        a = kks[j] * dec * jnp.concatenate([bb, bb], axis=1)
        b0 = a * nbase_ref[...]
        sq.append(b0)
        t_inv.append(eye + b0)
        a_bf = a.astype(BF16)
        e_lo.append([a_bf * m_ref[...] for m_ref in (e1_ref, e2_ref)])
        rhs.append(jnp.concatenate([v_all[:, hs] * bb, ks[j] * (bb * e_gam[:, hs])], axis=1))
        qkm.append((qks[j] * dec).astype(BF16) * incl_bf)
        gbs.append(gb)

    n_sq = INV_BASE.bit_length() - 2
    for h in heads:
        sq[h] = _bdot(sq[h], sq[h])
    for lvl in range(n_sq):
        for h in heads:
            if lvl == n_sq - 1:
                t_inv[h] = t_inv[h] + _bdot(t_inv[h], sq[h])
            else:
                res = _bdot(jnp.concatenate([t_inv[h], sq[h]], axis=0), sq[h])
                t_inv[h] = t_inv[h] + res[:rows]
                sq[h] = res[rows:]
    for lvl in range(len(e_lo[0])):
        for h in heads:
            t_inv[h] = t_inv[h] - _bdot(t_inv[h], _bdot(e_lo[h][lvl], t_inv[h]))
    xs = [_bdot(t_inv[h], rhs[h]) for h in heads]

    qd = [qs[h // GDN_V_PER_QK] * e_gam[:, h * dv:(h + 1) * dv] for h in heads]
    qs_parts = [[] for _ in heads]
    vn_parts = [[] for _ in heads]
    for ci in range(rows // CHUNK):
        sl = slice(ci * CHUNK, (ci + 1) * CHUNK)
        for h in heads:
            j = h // GDN_V_PER_QK
            u_c, w_c = xs[h][sl, :dv], xs[h][sl, dv:]
            gb = gbs[h]
            gl = gb[ci * CHUNK + CHUNK - 1:(ci + 1) * CHUNK, :]
            state = st_ref[h]
            wq = _bdot(jnp.concatenate([w_c, qd[h][sl]], axis=0), state)
            v_new = u_c - wq[:CHUNK]
            k_tail = ks[j][sl] * jnp.exp(gl - gb[sl])
            st_ref[h] = state * jnp.exp(gl) + _bdot_tn(k_tail, v_new)
            qs_parts[h].append(wq[CHUNK:])
            vn_parts[h].append(v_new)
    z = z_ref[...]
    for h in heads:
        hs = slice(h * dv, (h + 1) * dv)
        o = jnp.concatenate(qs_parts[h], axis=0) + jnp.dot(
            qkm[h], jnp.concatenate(vn_parts[h], axis=0).astype(BF16), preferred_element_type=F32)
        o = _rms(o) * nw_ref[...] * z[:, hs]
        o_ref[:, hs] = o.astype(o_ref.dtype)


def _gdn(proj, p1, p2, p2t, e_beta, e_gam, conv_w, norm_w, bsz, seq, offs, n_qk, g_row0):
    dk = GDN_HEAD_DIM
    qb = GDN_QK_PER_STEP
    vb = qb * GDN_V_PER_QK
    qw = qb * dk
    vw = vb * GDN_HEAD_DIM
    ts = SEQ_TILE
    nt = seq // ts
    q0 = offs["gdn_q"] // qw
    k0 = offs["gdn_k"] // qw
    v0 = offs["gdn_v"] // vw
    z0 = offs["gdn_z"] // vw
    ck0 = (n_qk * dk) // qw
    cv0 = (2 * n_qk * dk) // vw
    assert g_row0 % SUBLANES == 0 and (SUBLANES % vb == 0 or vb % SUBLANES == 0)
    row = lambda b, g, i: b * nt + i
    r = jnp.arange(ts)
    same = r[:, None] // CHUNK == r[None, :] // CHUNK
    lower = r[:, None] > r[None, :]
    incl = (same & (r[:, None] >= r[None, :])).astype(BF16)
    eye = (r[:, None] == r[None, :]).astype(F32)
    blk = lambda size: r[:, None] // size == r[None, :] // size
    nbase = -(blk(INV_BASE) & lower).astype(F32)
    e_masks = []
    size = INV_BASE
    while size < CHUNK:
        e_masks.append((blk(2 * size) & ~blk(size) & lower).astype(BF16))
        size *= 2
    assert len(e_masks) == 2
    return pl.pallas_call(
        _gdn_kernel,
        grid=(bsz, n_qk // qb, nt),
        in_specs=[
            pl.BlockSpec((ts, qw), lambda b, g, i: (row(b, g, i), q0 + g)),
            pl.BlockSpec((ts, qw), lambda b, g, i: (row(b, g, i), k0 + g)),
            pl.BlockSpec((ts, vw), lambda b, g, i: (row(b, g, i), v0 + g)),
            pl.BlockSpec((ts, vw), lambda b, g, i: (row(b, g, i), z0 + g)),
            pl.BlockSpec((ts, SMALL_W), lambda b, g, i: (row(b, g, i), 0)),
            pl.BlockSpec((ts, SMALL_W), lambda b, g, i: (row(b, g, i), 0)),
            pl.BlockSpec((max(vb, SUBLANES), ts),
                         lambda b, g, i: ((g_row0 + g * vb) // max(vb, SUBLANES), row(b, g, i))),
            pl.BlockSpec((None, SMALL_W, vw), lambda b, g, i: (g, 0, 0)),
            pl.BlockSpec((None, SMALL_W, vw), lambda b, g, i: (g, 0, 0)),
            pl.BlockSpec((ts, ts), lambda b, g, i: (0, 0)),
            pl.BlockSpec((ts, ts), lambda b, g, i: (0, 0)),
            pl.BlockSpec((ts, ts), lambda b, g, i: (0, 0)),
            pl.BlockSpec((ts, ts), lambda b, g, i: (0, 0)),
            pl.BlockSpec((ts, ts), lambda b, g, i: (0, 0)),
            pl.BlockSpec((CONV_K, qw), lambda b, g, i: (0, g)),
            pl.BlockSpec((CONV_K, qw), lambda b, g, i: (0, ck0 + g)),
            pl.BlockSpec((CONV_K, vw), lambda b, g, i: (0, cv0 + g)),
            pl.BlockSpec((1, dk), lambda b, g, i: (0, 0)),
        ],
        out_specs=pl.BlockSpec((ts, vw), lambda b, g, i: (row(b, g, i), g)),
        out_shape=jax.ShapeDtypeStruct((bsz * seq, n_qk * GDN_V_PER_QK * GDN_HEAD_DIM), BF16),
        scratch_shapes=[pltpu.VMEM((ts + HALO, qw), F32),
                        pltpu.VMEM((ts + HALO, qw), F32),
                        pltpu.VMEM((ts + HALO, vw), F32),
                        pltpu.VMEM((vb, dk, GDN_HEAD_DIM), F32)],
        compiler_params=_cparams(("arbitrary", "arbitrary", "arbitrary")),
        name="gdn_mixer",
    )(proj, proj, proj, proj, p1, p2, p2t, e_beta, e_gam, incl, eye, nbase, e_masks[0], e_masks[1],
      conv_w, conv_w, conv_w, norm_w)


def _merge_kernel(ys_ref, yg_ref, gs_ref, gg_ref, ws_ref, wg_ref, o_ref):
    a = jnp.dot(ys_ref[...], ws_ref[...], preferred_element_type=F32)
    b = jnp.dot(yg_ref[...], wg_ref[...], preferred_element_type=F32)
    o_ref[...] = (gs_ref[...] * a + gg_ref[...] * b).astype(o_ref.dtype)


def _merge(y_ssm, y_gdn, proj, w_ssm, w_gdn, offs):
    m, k = y_ssm.shape
    n = w_ssm.shape[1]
    tm, tn = 512, 512
    gs0 = offs["gate_ssm"] // tn
    gg0 = offs["gate_gdn"] // tn
    return pl.pallas_call(
        _merge_kernel,
        grid=(n // tn, m // tm),
        in_specs=[pl.BlockSpec((tm, k), lambda j, i: (i, 0)),
                  pl.BlockSpec((tm, k), lambda j, i: (i, 0)),
                  pl.BlockSpec((tm, tn), lambda j, i: (i, gs0 + j)),
                  pl.BlockSpec((tm, tn), lambda j, i: (i, gg0 + j)),
                  pl.BlockSpec((k, tn), lambda j, i: (0, j)),
                  pl.BlockSpec((k, tn), lambda j, i: (0, j))],
        out_specs=pl.BlockSpec((tm, tn), lambda j, i: (i, j)),
        out_shape=jax.ShapeDtypeStruct((m, n), BF16),
        compiler_params=_cparams(("arbitrary", "arbitrary")),
        name="merge_proj",
    )(y_ssm, y_gdn, proj, proj, w_ssm, w_gdn)


def _mixout_kernel(m_ref, wo_ref, x_ref, gate_ref, nw_ref, shift_ref, scale_ref, x1_ref, h2_ref):
    mix = jnp.dot(m_ref[...], wo_ref[...], preferred_element_type=F32)
    x1 = x_ref[...] + gate_ref[...] * mix
    x1_ref[...] = x1
    y = _rms(x1) * nw_ref[...]
    h2_ref[...] = (y * (1.0 + scale_ref[...]) + shift_ref[...]).astype(h2_ref.dtype)


def _mixout(merged, w_o, x, mod3, norm_w, gate_idx, shift_idx, scale_idx):
    bsz, seq, d = x.shape
    tm = 512
    nt = seq // tm
    return pl.pallas_call(
        _mixout_kernel,
        grid=(bsz, nt),
        in_specs=[pl.BlockSpec((tm, d), lambda b, i: (b * nt + i, 0)),
                  pl.BlockSpec((d, d), lambda b, i: (0, 0)),
                  pl.BlockSpec((None, tm, d), lambda b, i: (b, i, 0)),
                  pl.BlockSpec((None, 1, d), lambda b, i: (b, 0, gate_idx)),
                  pl.BlockSpec((1, d), lambda b, i: (0, 0)),
                  pl.BlockSpec((None, 1, d), lambda b, i: (b, 0, shift_idx)),
                  pl.BlockSpec((None, 1, d), lambda b, i: (b, 0, scale_idx))],
        out_specs=[pl.BlockSpec((None, tm, d), lambda b, i: (b, i, 0)),
                   pl.BlockSpec((None, tm, d), lambda b, i: (b, i, 0))],
        out_shape=[jax.ShapeDtypeStruct((bsz, seq, d), F32),
                   jax.ShapeDtypeStruct((bsz, seq, d), BF16)],
        compiler_params=_cparams(("arbitrary", "arbitrary")),
        name="mix_out",
    )(merged, w_o, x, mod3, norm_w.reshape(1, d), mod3, mod3)


def _gateup_kernel(h_ref, wg_ref, wu_ref, o_ref, wgb_ref, wub_ref):
    @pl.when(pl.program_id(1) == 0)
    def _():
        wgb_ref[...] = wg_ref[...].astype(BF16)
        wub_ref[...] = wu_ref[...].astype(BF16)

    h = h_ref[...]
    g = jnp.dot(h, wgb_ref[...], preferred_element_type=F32)
    u = jnp.dot(h, wub_ref[...], preferred_element_type=F32)
    o_ref[...] = (_silu(g) * u).astype(o_ref.dtype)


def _gateup(h, w_gate_up, layer):
    m, k = h.shape
    hid = w_gate_up.shape[2] // 2
    tm, tn = 1024, 512
    up0 = hid // tn
    return pl.pallas_call(
        _gateup_kernel,
        grid=(hid // tn, m // tm),
        in_specs=[pl.BlockSpec((tm, k), lambda j, i: (i, 0)),
                  pl.BlockSpec((None, k, tn), lambda j, i: (layer, 0, j)),
                  pl.BlockSpec((None, k, tn), lambda j, i: (layer, 0, up0 + j))],
        out_specs=pl.BlockSpec((tm, tn), lambda j, i: (i, j)),
        out_shape=jax.ShapeDtypeStruct((m, hid), BF16),
        scratch_shapes=[pltpu.VMEM((k, tn), BF16), pltpu.VMEM((k, tn), BF16)],
        compiler_params=_cparams(("arbitrary", "arbitrary")),
        name="ffn_gate_up",
    )(h, w_gate_up, w_gate_up)


def _down_kernel(a_ref, wd_ref, x1_ref, gate_ref, fw_ref, o_ref, *, final):
    ffn = jnp.dot(a_ref[...], wd_ref[...], preferred_element_type=F32)
    x2 = x1_ref[...] + gate_ref[...] * ffn
    if final:
        x2 = _rms(x2) * fw_ref[...]
    o_ref[...] = x2


def _down(act, w_down, x1, mod3, final_w, gate_idx, final):
    bsz, seq, d = x1.shape
    hid = act.shape[1]
    tm = 256
    nt = seq // tm
    return pl.pallas_call(
        functools.partial(_down_kernel, final=final),
        grid=(bsz, nt),
        in_specs=[pl.BlockSpec((tm, hid), lambda b, i: (b * nt + i, 0)),
                  pl.BlockSpec((hid, d), lambda b, i: (0, 0), pipeline_mode=pl.Buffered(1)),
                  pl.BlockSpec((None, tm, d), lambda b, i: (b, i, 0)),
                  pl.BlockSpec((None, 1, d), lambda b, i: (b, 0, gate_idx)),
                  pl.BlockSpec((1, d), lambda b, i: (0, 0))],
        out_specs=pl.BlockSpec((None, tm, d), lambda b, i: (b, i, 0)),
        out_shape=jax.ShapeDtypeStruct((bsz, seq, d), F32),
        compiler_params=_cparams(("arbitrary", "arbitrary")),
        name="ffn_down",
    )(act, w_down, x1, mod3, final_w.reshape(1, d))


def _selector(n_groups, src0, per_group, width, rep):
    k = jnp.arange(SMALL_W)[None, :, None]
    g = jnp.arange(n_groups)[:, None, None]
    c = jnp.arange(width)[None, None, :]
    return (k == src0 + g * per_group + c // rep).astype(BF16)


def kernel(x, c, w_ada, b_ada, norm_mix_w, w_in, ssm_conv_w, ssm_conv_b, ssm_dt_bias, ssm_a_log, ssm_d_skip, ssm_norm_w, gdn_conv_w, gdn_a_log, gdn_dt_bias, gdn_norm_w, w_ssm_proj, w_gdn_proj, w_o, norm_ffn_w, w_gate_up, w_down, final_norm_w):
    bsz, seq, d = x.shape
    depth = w_ada.shape[0]
    d_in = ssm_norm_w.shape[1]
    n_ssm_heads = ssm_dt_bias.shape[1]
    gn = SSM_N_GROUPS * SSM_D_STATE
    n_v = gdn_a_log.shape[1]
    n_qk = n_v // GDN_V_PER_QK
    qk_dim = n_qk * GDN_HEAD_DIM
    v_dim = n_v * GDN_HEAD_DIM
    assert n_ssm_heads + 2 * n_v == SMALL_W and seq % SEQ_TILE == 0 and seq % SSD_TILE == 0

    sizes = (d_in, d_in + 2 * gn, n_ssm_heads, 2 * qk_dim + v_dim, v_dim, n_v, n_v, d, d)
    starts = [0]
    for s in sizes:
        starts.append(starts[-1] + s)
    (s_z, s_xbc, s_dt, s_qkv, s_gz, s_beta, s_a, s_gs, s_gg) = starts[:-1]
    offs = {"ssm_z": 0, "ssm_x": d_in, "ssm_b": 2 * d_in, "ssm_c": 2 * d_in + gn}
    o_qkv = 2 * d_in + 2 * gn
    offs.update({"gdn_q": o_qkv, "gdn_k": o_qkv + qk_dim, "gdn_v": o_qkv + 2 * qk_dim})
    offs["gdn_z"] = o_qkv + 2 * qk_dim + v_dim
    offs["gate_ssm"] = offs["gdn_z"] + v_dim
    offs["gate_gdn"] = offs["gate_ssm"] + d

    vb = GDN_QK_PER_STEP * GDN_V_PER_QK
    e_dt = _selector(SSM_N_GROUPS, 0, n_ssm_heads // SSM_N_GROUPS, d_in // SSM_N_GROUPS, SSM_HEAD_DIM)
    e_beta = _selector(n_v // vb, n_ssm_heads, vb, vb * GDN_HEAD_DIM, GDN_HEAD_DIM)
    e_gam = _selector(n_v // vb, n_ssm_heads + n_v, vb, vb * GDN_HEAD_DIM, GDN_HEAD_DIM)

    out = x
    for layer in range(depth):
        wl = w_in[layer]
        w_small = jnp.concatenate([wl[:, s_dt:s_qkv], wl[:, s_beta:s_gs]], axis=1).astype(BF16)

        mod = _adaln(c, w_ada[layer], b_ada[layer])
        mod3 = mod.reshape(bsz, 1, 6 * d)
        h = _norm_mod(out, norm_mix_w[layer], mod3, 0, 1).reshape(bsz * seq, d)
        proj = _inproj(h, w_in, layer, offs, offs["gate_gdn"] + d, s_qkv - s_dt, s_gs - s_gz - v_dim + s_qkv - s_dt)
        small = _matmul(h, w_small, 1024, SMALL_W, F32, "in_proj_small")

        zeros_b = jnp.zeros((n_v,), F32)
        bias = jnp.concatenate([ssm_dt_bias[layer], zeros_b, gdn_dt_bias[layer]])[None, :]
        alog = jnp.concatenate([ssm_a_log[layer], zeros_b, gdn_a_log[layer]])[None, :]
        p1, p2, p2t = _prep(small, bias, alog, n_ssm_heads, n_v)

        y_ssm = _ssd(proj, p1, p2, p2t, e_dt, ssm_conv_w[layer], ssm_conv_b[layer], ssm_d_skip[layer],
                     ssm_norm_w[layer], bsz, seq, offs)
        y_gdn = _gdn(proj, p1, p2, p2t, e_beta, e_gam, gdn_conv_w[layer], gdn_norm_w[layer][None, :],
                     bsz, seq, offs, n_qk, n_ssm_heads + n_v)

        merged = _merge(y_ssm, y_gdn, proj, w_ssm_proj[layer].astype(BF16), w_gdn_proj[layer].astype(BF16), offs)
        x1, h2 = _mixout(merged, w_o[layer].astype(BF16), out, mod3, norm_ffn_w[layer], 2, 3, 4)
        act = _gateup(h2.reshape(bsz * seq, d), w_gate_up, layer)
        out = _down(act, w_down[layer].astype(BF16), x1, mod3, final_norm_w, 5, layer == depth - 1)
    return out
```

```python
import functools

import jax
import jax.numpy as jnp
from jax import lax
from jax.experimental import pallas as pl
from jax.experimental.pallas import tpu as pltpu

F32 = jnp.float32
BF16 = jnp.bfloat16

NORM_EPS = 1e-6
L2_EPS = 1e-6
CONV_K = 4
SUBLANES = 8
HALO = 8
CHUNK = 64
INV_BASE = 16
SEQ_TILE = 256
SSD_TILE = 1024
SSM_HEAD_DIM = 64
SSM_N_GROUPS = 8
SSM_D_STATE = 128
GDN_HEAD_DIM = 128
GDN_V_PER_QK = 2
GDN_QK_PER_STEP = 4
SMALL_W = 128
INPROJ_ROWS = 256
VMEM_LIMIT = 56 * 1024 * 1024


def _cparams(sem):
    return pltpu.CompilerParams(dimension_semantics=sem, vmem_limit_bytes=VMEM_LIMIT)


def _bdot(a, b):
    return jnp.dot(a.astype(BF16), b.astype(BF16), preferred_element_type=F32)


def _bdot_nt(a, b):
    return lax.dot_general(a.astype(BF16), b.astype(BF16), (((1,), (1,)), ((), ())),
                           preferred_element_type=F32)


def _bdot_tn(a, b):
    return lax.dot_general(a.astype(BF16), b.astype(BF16), (((0,), (0,)), ((), ())),
                           preferred_element_type=F32)


def _split3(x):
    hi = x.astype(BF16)
    r1 = x - hi.astype(F32)
    mid = r1.astype(BF16)
    lo = (r1 - mid.astype(F32)).astype(BF16)
    return hi, mid, lo


def _dot01_left(m01, x):
    hi, mid, lo = _split3(x)
    d = lambda p: jnp.dot(m01, p, preferred_element_type=F32)
    return (d(lo) + d(mid)) + d(hi)


def _dot01_right(x, m01):
    hi, mid, lo = _split3(x)
    d = lambda p: jnp.dot(p, m01, preferred_element_type=F32)
    return (d(lo) + d(mid)) + d(hi)


def _dot01_tn(x, m01):
    hi, mid, lo = _split3(x)
    d = lambda p: lax.dot_general(p, m01, (((0,), (0,)), ((), ())), preferred_element_type=F32)
    return (d(lo) + d(mid)) + d(hi)


def _sigmoid(x):
    return 1.0 / (1.0 + jnp.exp(-x))


def _silu(x):
    return x * _sigmoid(x)


def _silu_tanh(x):
    h = 0.5 * x
    return h * jnp.tanh(h) + h


def _softplus(x):
    return jnp.maximum(x, 0.0) + jnp.log(1.0 + jnp.exp(-jnp.abs(x)))


def _rms(x, eps=NORM_EPS):
    return x * lax.rsqrt(jnp.mean(x * x, axis=-1, keepdims=True) + eps)


def _conv_silu(stage_ref, cur, w_ref, bias, rows):
    stage_ref[HALO:HALO + rows, :] = cur
    acc = None
    for k in reversed(range(CONV_K)):
        start = HALO - (CONV_K - 1) + k
        term = stage_ref[start:start + rows, :] * w_ref[k:k + 1, :]
        acc = term if acc is None else acc + term
    if bias is not None:
        acc = acc + bias
    stage_ref[0:HALO, :] = cur[rows - HALO:rows, :]
    return _silu_tanh(acc)


def _ada_kernel(c_ref, w_ref, b_ref, o_ref):
    c_act = _silu(c_ref[...])
    o_ref[...] = _bdot(c_act, w_ref[...]) + b_ref[...]


def _adaln(c, w_ada, b_ada):
    bsz, d = c.shape
    n = w_ada.shape[1]
    tn = 1024
    return pl.pallas_call(
        _ada_kernel,
        grid=(n // tn,),
        in_specs=[pl.BlockSpec((bsz, d), lambda j: (0, 0)),
                  pl.BlockSpec((d, tn), lambda j: (0, j)),
                  pl.BlockSpec((1, tn), lambda j: (0, j))],
        out_specs=pl.BlockSpec((bsz, tn), lambda j: (0, j)),
        out_shape=jax.ShapeDtypeStruct((bsz, n), F32),
        compiler_params=_cparams(("arbitrary",)),
        name="adaln",
    )(c, w_ada, b_ada.reshape(1, n))


def _norm_mod_kernel(x_ref, w_ref, shift_ref, scale_ref, o_ref):
    y = _rms(x_ref[...]) * w_ref[...]
    o_ref[...] = (y * (1.0 + scale_ref[...]) + shift_ref[...]).astype(o_ref.dtype)


def _norm_mod(x, w, mod3, shift_idx, scale_idx):
    bsz, seq, d = x.shape
    tm = 512
    return pl.pallas_call(
        _norm_mod_kernel,
        grid=(bsz, seq // tm),
        in_specs=[pl.BlockSpec((None, tm, d), lambda b, i: (b, i, 0)),
                  pl.BlockSpec((1, d), lambda b, i: (0, 0)),
                  pl.BlockSpec((None, 1, d), lambda b, i: (b, 0, shift_idx)),
                  pl.BlockSpec((None, 1, d), lambda b, i: (b, 0, scale_idx))],
        out_specs=pl.BlockSpec((None, tm, d), lambda b, i: (b, i, 0)),
        out_shape=jax.ShapeDtypeStruct((bsz, seq, d), BF16),
        compiler_params=_cparams(("arbitrary", "arbitrary")),
        name="norm_mod",
    )(x, w.reshape(1, d), mod3, mod3)


def _mm_kernel(x_ref, w_ref, o_ref):
    w = w_ref[...].astype(x_ref.dtype)
    o_ref[...] = jnp.dot(x_ref[...], w, preferred_element_type=F32).astype(o_ref.dtype)


def _matmul(x, w, tm, tn, out_dtype, name):
    m, k = x.shape
    n = w.shape[1]
    return pl.pallas_call(
        _mm_kernel,
        grid=(n // tn, m // tm),
        in_specs=[pl.BlockSpec((tm, k), lambda j, i: (i, 0)),
                  pl.BlockSpec((k, tn), lambda j, i: (0, j))],
        out_specs=pl.BlockSpec((tm, tn), lambda j, i: (i, j)),
        out_shape=jax.ShapeDtypeStruct((m, n), out_dtype),
        compiler_params=_cparams(("arbitrary", "arbitrary")),
        name=name,
    )(x, w)


def _inproj_kernel(x_ref, wa_ref, wb_ref, o_ref, wbf_ref, *, n_shift1, n_shift2, shift1, shift2):
    n = pl.program_id(0)
    kdim = x_ref.shape[1]

    @pl.when(pl.program_id(1) == 0)
    def _():
        def repack(shift):
            for r0 in range(0, kdim, INPROJ_ROWS):
                rs = slice(r0, r0 + INPROJ_ROWS)
                if shift == 0:
                    w = wa_ref[rs, :]
                else:
                    w = jnp.concatenate([wa_ref[rs, shift:], wb_ref[rs, :shift]], axis=1)
                wbf_ref[rs, :] = w.astype(BF16)

        @pl.when(n < n_shift1)
        def _():
            repack(0)

        @pl.when((n >= n_shift1) & (n < n_shift2))
        def _():
            repack(shift1)

        @pl.when(n >= n_shift2)
        def _():
            repack(shift2)

    o_ref[...] = jnp.dot(x_ref[...], wbf_ref[...], preferred_element_type=F32)


def _inproj(h, w_in, layer, offs, n_main, shift1, shift2):
    m, k = h.shape
    tm, tn = 1024, 1024
    segs = (offs["gate_ssm"], n_main, offs["gdn_q"])
    assert all(s % tn == 0 for s in segs) and shift2 == SMALL_W and 0 < shift1 < SMALL_W
    return pl.pallas_call(
        functools.partial(_inproj_kernel, n_shift1=offs["gdn_q"] // tn, n_shift2=offs["gate_ssm"] // tn,
                          shift1=shift1, shift2=shift2),
        grid=(n_main // tn, m // tm),
        in_specs=[pl.BlockSpec((tm, k), lambda j, i: (i, 0)),
                  pl.BlockSpec((None, k, tn), lambda j, i: (layer, 0, j)),
                  pl.BlockSpec((None, k, SMALL_W), lambda j, i: (layer, 0, (j + 1) * (tn // SMALL_W)))],
        out_specs=pl.BlockSpec((tm, tn), lambda j, i: (i, j)),
        out_shape=jax.ShapeDtypeStruct((m, n_main), F32),
        scratch_shapes=[pltpu.VMEM((k, tn), BF16)],
        compiler_params=_cparams(("arbitrary", "arbitrary")),
        name="in_proj",
    )(h, w_in, w_in)


def _prep_kernel(sm_ref, bias_ref, alog_ref, tril_ref, eye_ref, p1_ref, p2_ref, p2t_ref, *, n_dt, n_beta):
    sm = sm_ref[...]
    lane = lax.broadcasted_iota(jnp.int32, sm.shape, 1)
    is_dt = lane < n_dt
    is_beta = (lane >= n_dt) & (lane < n_dt + n_beta)
    sp = _softplus(sm + bias_ref[...])
    rate = -jnp.exp(alog_ref[...]) * sp
    p1_ref[...] = jnp.where(is_beta, _sigmoid(sm), jnp.where(is_dt, sp, rate))
    cum = _dot01_left(tril_ref[...], jnp.where(is_beta, 0.0, rate))
    p2_ref[...] = cum
    p2t_ref[...] = _dot01_tn(cum, eye_ref[...])


def _prep(small, bias, alog, n_dt, n_beta):
    t = small.shape[0]
    ts = SEQ_TILE
    r = jnp.arange(ts)
    tril = ((r[:, None] // CHUNK == r[None, :] // CHUNK) & (r[:, None] >= r[None, :])).astype(BF16)
    eye = (r[:, None] == r[None, :]).astype(BF16)
    return pl.pallas_call(
        functools.partial(_prep_kernel, n_dt=n_dt, n_beta=n_beta),
        grid=(t // ts,),
        in_specs=[pl.BlockSpec((ts, SMALL_W), lambda i: (i, 0)),
                  pl.BlockSpec((1, SMALL_W), lambda i: (0, 0)),
                  pl.BlockSpec((1, SMALL_W), lambda i: (0, 0)),
                  pl.BlockSpec((ts, ts), lambda i: (0, 0)),
                  pl.BlockSpec((ts, ts), lambda i: (0, 0))],
        out_specs=[pl.BlockSpec((ts, SMALL_W), lambda i: (i, 0)),
                   pl.BlockSpec((ts, SMALL_W), lambda i: (i, 0)),
                   pl.BlockSpec((SMALL_W, ts), lambda i: (0, i))],
        out_shape=[jax.ShapeDtypeStruct((t, SMALL_W), F32),
                   jax.ShapeDtypeStruct((t, SMALL_W), F32),
                   jax.ShapeDtypeStruct((SMALL_W, t), F32)],
        compiler_params=_cparams(("arbitrary",)),
        name="head_scalars",
    )(small, bias, alog, tril, eye)


def _ssd_kernel(z_ref, x_ref, b_ref, c_ref, p1_ref, p2_ref, arow_ref, e_ref, causal_ref, headbd_ref,
                xpar_ref, bpar_ref, cpar_ref, o_ref,
                xs_ref, bs_ref, cs_ref, st_ref):
    rows = x_ref.shape[0]
    gw = x_ref.shape[1]
    n_heads = gw // SSM_HEAD_DIM
    bias_row = slice(CONV_K, CONV_K + 1)
    dsk_row = slice(CONV_K + 1, CONV_K + 2)
    nw_row = slice(CONV_K + 2, CONV_K + 3)

    @pl.when(pl.program_id(2) == 0)
    def _():
        xs_ref[0:HALO, :] = jnp.zeros((HALO, gw), F32)
        bs_ref[0:HALO, :] = jnp.zeros((HALO, SSM_D_STATE), F32)
        cs_ref[0:HALO, :] = jnp.zeros((HALO, SSM_D_STATE), F32)
        st_ref[...] = jnp.zeros_like(st_ref)

    xa = _conv_silu(xs_ref, x_ref[...], xpar_ref, xpar_ref[bias_row, :], rows)
    ba = _conv_silu(bs_ref, b_ref[...], bpar_ref, bpar_ref[bias_row, :], rows)
    ca = _conv_silu(cs_ref, c_ref[...], cpar_ref, cpar_ref[bias_row, :], rows)

    e = e_ref[...]
    dt = _dot01_right(p1_ref[...], e)
    a_cum = _dot01_right(p2_ref[...], e)
    xdt = xa * dt
    ea = jnp.exp(a_cum)
    head_bd = headbd_ref[...]

    ys = []
    for ci in range(rows // CHUNK):
        sl = slice(ci * CHUNK, (ci + 1) * CHUNK)
        b_c, c_c = ba[sl], ca[sl]
        ac = a_cum[sl]
        a_last = ac[CHUNK - 1:CHUNK, :]
        xdt_c = xdt[sl]
        a_row = jnp.concatenate([arow_ref[h:h + 1, sl] for h in range(n_heads)], axis=1)
        decay = jnp.exp(jnp.minimum(ac - a_row, 0.0)) * causal_ref[...]
        cb = _bdot_nt(c_c, jnp.concatenate([b_c] * n_heads, axis=0))
        scores = cb * decay
        x_bd = jnp.concatenate([xdt_c.astype(BF16)] * n_heads, axis=0) * head_bd
        y_diag = jnp.dot(scores.astype(BF16), x_bd, preferred_element_type=F32)
        state = st_ref[...]
        y_off = _bdot(c_c, state) * ea[sl]
        xw = xdt_c * jnp.exp(a_last - ac)
        st_ref[...] = state * jnp.exp(a_last) + _bdot_tn(b_c, xw)
        ys.append(y_diag + y_off)
    y = jnp.concatenate(ys, axis=0) + xpar_ref[dsk_row, :] * xa
    y = y * _silu_tanh(z_ref[...])
    o_ref[...] = (_rms(y) * xpar_ref[nw_row, :]).astype(o_ref.dtype)


def _ssd(proj, p1, p2, p2t, e_sel, conv_w, conv_b, d_skip, norm_w, bsz, seq, offs):
    d_in = norm_w.shape[0]
    gw = d_in // SSM_N_GROUPS
    hpg = gw // SSM_HEAD_DIM
    n = SSM_D_STATE
    ts = SSD_TILE
    nt = seq // ts
    zrow = jnp.zeros_like(conv_b)
    xtra = lambda v: jnp.concatenate([v, jnp.zeros((conv_b.shape[0] - d_in,), F32)])
    par = jnp.concatenate([conv_w, conv_b[None, :], xtra(jnp.repeat(d_skip, SSM_HEAD_DIM))[None, :],
                           xtra(norm_w)[None, :], zrow[None, :]], axis=0)
    assert par.shape[0] == SUBLANES
    z0 = offs["ssm_z"] // gw
    x0 = offs["ssm_x"] // gw
    b0 = offs["ssm_b"] // n
    c0 = offs["ssm_c"] // n
    cb0 = d_in // n
    cc0 = (d_in + SSM_N_GROUPS * n) // n
    row = lambda b, g, i: b * nt + i
    r = jnp.arange(CHUNK)[:, None]
    l = jnp.arange(gw)[None, :]
    causal = (r >= l % SSM_HEAD_DIM).astype(F32)
    k = jnp.arange(gw)
    head_bd = (k[:, None] // SSM_HEAD_DIM == k[None, :] // SSM_HEAD_DIM).astype(BF16)
    return pl.pallas_call(
        _ssd_kernel,
        grid=(bsz, SSM_N_GROUPS, nt),
        in_specs=[
            pl.BlockSpec((ts, gw), lambda b, g, i: (row(b, g, i), z0 + g)),
            pl.BlockSpec((ts, gw), lambda b, g, i: (row(b, g, i), x0 + g)),
            pl.BlockSpec((ts, n), lambda b, g, i: (row(b, g, i), b0 + g)),
            pl.BlockSpec((ts, n), lambda b, g, i: (row(b, g, i), c0 + g)),
            pl.BlockSpec((ts, SMALL_W), lambda b, g, i: (row(b, g, i), 0)),
            pl.BlockSpec((ts, SMALL_W), lambda b, g, i: (row(b, g, i), 0)),
            pl.BlockSpec((hpg, ts), lambda b, g, i: (g, row(b, g, i))),
            pl.BlockSpec((None, SMALL_W, gw), lambda b, g, i: (g, 0, 0)),
            pl.BlockSpec((CHUNK, gw), lambda b, g, i: (0, 0)),
            pl.BlockSpec((gw, gw), lambda b, g, i: (0, 0)),
            pl.BlockSpec((SUBLANES, gw), lambda b, g, i: (0, g)),
            pl.BlockSpec((SUBLANES, n), lambda b, g, i: (0, cb0 + g)),
            pl.BlockSpec((SUBLANES, n), lambda b, g, i: (0, cc0 + g)),
        ],
        out_specs=pl.BlockSpec((ts, gw), lambda b, g, i: (row(b, g, i), g)),
        out_shape=jax.ShapeDtypeStruct((bsz * seq, d_in), BF16),
        scratch_shapes=[pltpu.VMEM((ts + HALO, gw), F32),
                        pltpu.VMEM((ts + HALO, n), F32),
                        pltpu.VMEM((ts + HALO, n), F32),
                        pltpu.VMEM((n, gw), F32)],
        compiler_params=_cparams(("arbitrary", "arbitrary", "arbitrary")),
        name="ssd_mixer",
    )(proj, proj, proj, proj, p1, p2, p2t, e_sel, causal, head_bd, par, par, par)


def _gdn_kernel(q_ref, k_ref, v_ref, z_ref, p1_ref, p2_ref, grow_ref, eb_ref, eg_ref,
                incl_ref, eye_ref, nbase_ref, e1_ref, e2_ref,
                cwq_ref, cwk_ref, cwv_ref, nw_ref, o_ref,
                qs_ref, ks_ref, vs_ref, st_ref):
    rows = q_ref.shape[0]
    dk = dv = GDN_HEAD_DIM
    n_qk = q_ref.shape[1] // dk
    n_v = v_ref.shape[1] // dv
    assert rows == 2 * dk

    @pl.when(pl.program_id(2) == 0)
    def _():
        qs_ref[0:HALO, :] = jnp.zeros((HALO, n_qk * dk), F32)
        ks_ref[0:HALO, :] = jnp.zeros((HALO, n_qk * dk), F32)
        vs_ref[0:HALO, :] = jnp.zeros((HALO, n_v * dv), F32)
        st_ref[...] = jnp.zeros_like(st_ref)

    q_all = _conv_silu(qs_ref, q_ref[...], cwq_ref, None, rows)
    k_all = _conv_silu(ks_ref, k_ref[...], cwk_ref, None, rows)
    v_all = _conv_silu(vs_ref, v_ref[...], cwv_ref, None, rows)
    beta = _dot01_right(p1_ref[...], eb_ref[...])
    gam = _dot01_right(p2_ref[...], eg_ref[...])
    e_gam = jnp.exp(gam)
    incl_bf = incl_ref[...]
    g_off = (pl.program_id(1) * n_v) % SUBLANES

    qs, ks, kks, qks = [], [], [], []
    for j in range(n_qk):
        js = slice(j * dk, (j + 1) * dk)
        q = q_all[:, js]
        k = k_all[:, js]
        q = q * (lax.rsqrt(jnp.sum(q * q, axis=-1, keepdims=True) + L2_EPS) * (dk ** -0.5))
        k = k * lax.rsqrt(jnp.sum(k * k, axis=-1, keepdims=True) + L2_EPS)
        qs.append(q)
        ks.append(k)
        kks.append(_bdot_nt(k, k))
        qks.append(_bdot_nt(q, k))

    heads = range(n_v)
    eye = eye_ref[...]
    t_inv, sq, e_lo, rhs, qkm, gbs = [], [], [], [], [], []
    for h in heads:
        j = h // GDN_V_PER_QK
        hs = slice(h * dv, (h + 1) * dv)
        gb = gam[:, hs]
        bb = beta[:, hs]
        seg = jnp.concatenate([gb, gb], axis=1) - grow_ref[pl.ds(g_off + h, 1), :]
        dec = jnp.exp(jnp.minimum(seg, 0.0))
        a = kks[j] * dec * jnp.concatenate([bb, bb], axis=1)
        b0 = a * nbase_ref[...]
        sq.append(b0)
        t_inv.append(eye + b0)
        a_bf = a.astype(BF16)
        e_lo.append([a_bf * m_ref[...] for m_ref in (e1_ref, e2_ref)])
        rhs.append(jnp.concatenate([v_all[:, hs] * bb, ks[j] * (bb * e_gam[:, hs])], axis=1))
        qkm.append((qks[j] * dec).astype(BF16) * incl_bf)
        gbs.append(gb)

    n_sq = INV_BASE.bit_length() - 2
    for h in heads:
        sq[h] = _bdot(sq[h], sq[h])
    for lvl in range(n_sq):
        for h in heads:
            if lvl == n_sq - 1:
                t_inv[h] = t_inv[h] + _bdot(t_inv[h], sq[h])
            else:
                res = _bdot(jnp.concatenate([t_inv[h], sq[h]], axis=0), sq[h])
                t_inv[h] = t_inv[h] + res[:rows]
                sq[h] = res[rows:]
    for lvl in range(len(e_lo[0])):
        for h in heads:
            t_inv[h] = t_inv[h] - _bdot(t_inv[h], _bdot(e_lo[h][lvl], t_inv[h]))
    xs = [_bdot(t_inv[h], rhs[h]) for h in heads]

    qd = [qs[h // GDN_V_PER_QK] * e_gam[:, h * dv:(h + 1) * dv] for h in heads]
    qs_parts = [[] for _ in heads]
    vn_parts = [[] for _ in heads]
    for ci in range(rows // CHUNK):
        sl = slice(ci * CHUNK, (ci + 1) * CHUNK)
        for h in heads:
            j = h // GDN_V_PER_QK
            u_c, w_c = xs[h][sl, :dv], xs[h][sl, dv:]
            gb = gbs[h]
            gl = gb[ci * CHUNK + CHUNK - 1:(ci + 1) * CHUNK, :]
            state = st_ref[h]
            wq = _bdot(jnp.concatenate([w_c, qd[h][sl]], axis=0), state)
            v_new = u_c - wq[:CHUNK]
            k_tail = ks[j][sl] * jnp.exp(gl - gb[sl])
            st_ref[h] = state * jnp.exp(gl) + _bdot_tn(k_tail, v_new)
            qs_parts[h].append(wq[CHUNK:])
            vn_parts[h].append(v_new)
    z = z_ref[...]
    for h in heads:
        hs = slice(h * dv, (h + 1) * dv)
        o = jnp.concatenate(qs_parts[h], axis=0) + jnp.dot(
            qkm[h], jnp.concatenate(vn_parts[h], axis=0).astype(BF16), preferred_element_type=F32)
        o = _rms(o) * nw_ref[...] * _silu_tanh(z[:, hs])
        o_ref[:, hs] = o.astype(o_ref.dtype)


def _gdn(proj, p1, p2, p2t, e_beta, e_gam, conv_w, norm_w, bsz, seq, offs, n_qk, g_row0):
    dk = GDN_HEAD_DIM
    qb = GDN_QK_PER_STEP
    vb = qb * GDN_V_PER_QK
    qw = qb * dk
    vw = vb * GDN_HEAD_DIM
    ts = SEQ_TILE
    nt = seq // ts
    q0 = offs["gdn_q"] // qw
    k0 = offs["gdn_k"] // qw
    v0 = offs["gdn_v"] // vw
    z0 = offs["gdn_z"] // vw
    ck0 = (n_qk * dk) // qw
    cv0 = (2 * n_qk * dk) // vw
    assert g_row0 % SUBLANES == 0 and (SUBLANES % vb == 0 or vb % SUBLANES == 0)
    row = lambda b, g, i: b * nt + i
    r = jnp.arange(ts)
    same = r[:, None] // CHUNK == r[None, :] // CHUNK
    lower = r[:, None] > r[None, :]
    incl = (same & (r[:, None] >= r[None, :])).astype(BF16)
    eye = (r[:, None] == r[None, :]).astype(F32)
    blk = lambda size: r[:, None] // size == r[None, :] // size
    nbase = -(blk(INV_BASE) & lower).astype(F32)
    e_masks = []
    size = INV_BASE
    while size < CHUNK:
        e_masks.append((blk(2 * size) & ~blk(size) & lower).astype(BF16))
        size *= 2
    assert len(e_masks) == 2
    return pl.pallas_call(
        _gdn_kernel,
        grid=(bsz, n_qk // qb, nt),
        in_specs=[
            pl.BlockSpec((ts, qw), lambda b, g, i: (row(b, g, i), q0 + g)),
            pl.BlockSpec((ts, qw), lambda b, g, i: (row(b, g, i), k0 + g)),
            pl.BlockSpec((ts, vw), lambda b, g, i: (row(b, g, i), v0 + g)),
            pl.BlockSpec((ts, vw), lambda b, g, i: (row(b, g, i), z0 + g)),
            pl.BlockSpec((ts, SMALL_W), lambda b, g, i: (row(b, g, i), 0)),
            pl.BlockSpec((ts, SMALL_W), lambda b, g, i: (row(b, g, i), 0)),
            pl.BlockSpec((max(vb, SUBLANES), ts),
                         lambda b, g, i: ((g_row0 + g * vb) // max(vb, SUBLANES), row(b, g, i))),
            pl.BlockSpec((None, SMALL_W, vw), lambda b, g, i: (g, 0, 0)),
            pl.BlockSpec((None, SMALL_W, vw), lambda b, g, i: (g, 0, 0)),
            pl.BlockSpec((ts, ts), lambda b, g, i: (0, 0)),
            pl.BlockSpec((ts, ts), lambda b, g, i: (0, 0)),
            pl.BlockSpec((ts, ts), lambda b, g, i: (0, 0)),
            pl.BlockSpec((ts, ts), lambda b, g, i: (0, 0)),
            pl.BlockSpec((ts, ts), lambda b, g, i: (0, 0)),
            pl.BlockSpec((CONV_K, qw), lambda b, g, i: (0, g)),
            pl.BlockSpec((CONV_K, qw), lambda b, g, i: (0, ck0 + g)),
            pl.BlockSpec((CONV_K, vw), lambda b, g, i: (0, cv0 + g)),
            pl.BlockSpec((1, dk), lambda b, g, i: (0, 0)),
        ],
        out_specs=pl.BlockSpec((ts, vw), lambda b, g, i: (row(b, g, i), g)),
        out_shape=jax.ShapeDtypeStruct((bsz * seq, n_qk * GDN_V_PER_QK * GDN_HEAD_DIM), BF16),
        scratch_shapes=[pltpu.VMEM((ts + HALO, qw), F32),
                        pltpu.VMEM((ts + HALO, qw), F32),
                        pltpu.VMEM((ts + HALO, vw), F32),
                        pltpu.VMEM((vb, dk, GDN_HEAD_DIM), F32)],
        compiler_params=_cparams(("arbitrary", "arbitrary", "arbitrary")),
        name="gdn_mixer",
    )(proj, proj, proj, proj, p1, p2, p2t, e_beta, e_gam, incl, eye, nbase, e_masks[0], e_masks[1],
      conv_w, conv_w, conv_w, norm_w)


def _merge_kernel(ys_ref, yg_ref, gs_ref, gg_ref, ws_ref, wg_ref, o_ref):
    a = jnp.dot(ys_ref[...], ws_ref[...], preferred_element_type=F32)
    b = jnp.dot(yg_ref[...], wg_ref[...], preferred_element_type=F32)
    o_ref[...] = (_sigmoid(gs_ref[...]) * a + _sigmoid(gg_ref[...]) * b).astype(o_ref.dtype)


def _merge(y_ssm, y_gdn, proj, w_ssm, w_gdn, offs):
    m, k = y_ssm.shape
    n = w_ssm.shape[1]
    tm, tn = 512, 512
    gs0 = offs["gate_ssm"] // tn
    gg0 = offs["gate_gdn"] // tn
    return pl.pallas_call(
        _merge_kernel,
        grid=(n // tn, m // tm),
        in_specs=[pl.BlockSpec((tm, k), lambda j, i: (i, 0)),
                  pl.BlockSpec((tm, k), lambda j, i: (i, 0)),
                  pl.BlockSpec((tm, tn), lambda j, i: (i, gs0 + j)),
                  pl.BlockSpec((tm, tn), lambda j, i: (i, gg0 + j)),
                  pl.BlockSpec((k, tn), lambda j, i: (0, j)),
                  pl.BlockSpec((k, tn), lambda j, i: (0, j))],
        out_specs=pl.BlockSpec((tm, tn), lambda j, i: (i, j)),
        out_shape=jax.ShapeDtypeStruct((m, n), BF16),
        compiler_params=_cparams(("arbitrary", "arbitrary")),
        name="merge_proj",
    )(y_ssm, y_gdn, proj, proj, w_ssm, w_gdn)


def _mixout_kernel(m_ref, wo_ref, x_ref, gate_ref, nw_ref, shift_ref, scale_ref, x1_ref, h2_ref):
    mix = jnp.dot(m_ref[...], wo_ref[...], preferred_element_type=F32)
    x1 = x_ref[...] + gate_ref[...] * mix
    x1_ref[...] = x1
    y = _rms(x1) * nw_ref[...]
    h2_ref[...] = (y * (1.0 + scale_ref[...]) + shift_ref[...]).astype(h2_ref.dtype)


def _mixout(merged, w_o, x, mod3, norm_w, gate_idx, shift_idx, scale_idx):
    bsz, seq, d = x.shape
    tm = 512
    nt = seq // tm
    return pl.pallas_call(
        _mixout_kernel,
        grid=(bsz, nt),
        in_specs=[pl.BlockSpec((tm, d), lambda b, i: (b * nt + i, 0)),
                  pl.BlockSpec((d, d), lambda b, i: (0, 0)),
                  pl.BlockSpec((None, tm, d), lambda b, i: (b, i, 0)),
                  pl.BlockSpec((None, 1, d), lambda b, i: (b, 0, gate_idx)),
                  pl.BlockSpec((1, d), lambda b, i: (0, 0)),
                  pl.BlockSpec((None, 1, d), lambda b, i: (b, 0, shift_idx)),
                  pl.BlockSpec((None, 1, d), lambda b, i: (b, 0, scale_idx))],
        out_specs=[pl.BlockSpec((None, tm, d), lambda b, i: (b, i, 0)),
                   pl.BlockSpec((None, tm, d), lambda b, i: (b, i, 0))],
        out_shape=[jax.ShapeDtypeStruct((bsz, seq, d), F32),
                   jax.ShapeDtypeStruct((bsz, seq, d), BF16)],
        compiler_params=_cparams(("arbitrary", "arbitrary")),
        name="mix_out",
    )(merged, w_o, x, mod3, norm_w.reshape(1, d), mod3, mod3)


def _gateup_kernel(h_ref, wg_ref, wu_ref, o_ref, wgb_ref, wub_ref):
    @pl.when(pl.program_id(1) == 0)
    def _():
        wgb_ref[...] = wg_ref[...].astype(BF16)
        wub_ref[...] = wu_ref[...].astype(BF16)

    h = h_ref[...]
    g = jnp.dot(h, wgb_ref[...], preferred_element_type=F32)
    u = jnp.dot(h, wub_ref[...], preferred_element_type=F32)
    o_ref[...] = (_silu(g) * u).astype(o_ref.dtype)


def _gateup(h, w_gate_up, layer):
    m, k = h.shape
    hid = w_gate_up.shape[2] // 2
    tm, tn = 1024, 512
    up0 = hid // tn
    return pl.pallas_call(
        _gateup_kernel,
        grid=(hid // tn, m // tm),
        in_specs=[pl.BlockSpec((tm, k), lambda j, i: (i, 0)),
                  pl.BlockSpec((None, k, tn), lambda j, i: (layer, 0, j)),
                  pl.BlockSpec((None, k, tn), lambda j, i: (layer, 0, up0 + j))],
        out_specs=pl.BlockSpec((tm, tn), lambda j, i: (i, j)),
        out_shape=jax.ShapeDtypeStruct((m, hid), BF16),
        scratch_shapes=[pltpu.VMEM((k, tn), BF16), pltpu.VMEM((k, tn), BF16)],
        compiler_params=_cparams(("arbitrary", "arbitrary")),
        name="ffn_gate_up",
    )(h, w_gate_up, w_gate_up)


def _down_kernel(a_ref, wd_ref, x1_ref, gate_ref, fw_ref, o_ref, *, final):
    ffn = jnp.dot(a_ref[...], wd_ref[...], preferred_element_type=F32)
    x2 = x1_ref[...] + gate_ref[...] * ffn
    if final:
        x2 = _rms(x2) * fw_ref[...]
    o_ref[...] = x2


def _down(act, w_down, x1, mod3, final_w, gate_idx, final):
    bsz, seq, d = x1.shape
    hid = act.shape[1]
    tm = 256
    nt = seq // tm
    return pl.pallas_call(
        functools.partial(_down_kernel, final=final),
        grid=(bsz, nt),
        in_specs=[pl.BlockSpec((tm, hid), lambda b, i: (b * nt + i, 0)),
                  pl.BlockSpec((hid, d), lambda b, i: (0, 0), pipeline_mode=pl.Buffered(1)),
                  pl.BlockSpec((None, tm, d), lambda b, i: (b, i, 0)),
                  pl.BlockSpec((None, 1, d), lambda b, i: (b, 0, gate_idx)),
                  pl.BlockSpec((1, d), lambda b, i: (0, 0))],
        out_specs=pl.BlockSpec((None, tm, d), lambda b, i: (b, i, 0)),
        out_shape=jax.ShapeDtypeStruct((bsz, seq, d), F32),
        compiler_params=_cparams(("arbitrary", "arbitrary")),
        name="ffn_down",
    )(act, w_down, x1, mod3, final_w.reshape(1, d))


def _selector(n_groups, src0, per_group, width, rep):
    k = jnp.arange(SMALL_W)[None, :, None]
    g = jnp.arange(n_groups)[:, None, None]
    c = jnp.arange(width)[None, None, :]
    return (k == src0 + g * per_group + c // rep).astype(BF16)


def kernel(x, c, w_ada, b_ada, norm_mix_w, w_in, ssm_conv_w, ssm_conv_b, ssm_dt_bias, ssm_a_log, ssm_d_skip, ssm_norm_w, gdn_conv_w, gdn_a_log, gdn_dt_bias, gdn_norm_w, w_ssm_proj, w_gdn_proj, w_o, norm_ffn_w, w_gate_up, w_down, final_norm_w):
    bsz, seq, d = x.shape
    depth = w_ada.shape[0]
    d_in = ssm_norm_w.shape[1]
    n_ssm_heads = ssm_dt_bias.shape[1]
    gn = SSM_N_GROUPS * SSM_D_STATE
    n_v = gdn_a_log.shape[1]
    n_qk = n_v // GDN_V_PER_QK
    qk_dim = n_qk * GDN_HEAD_DIM
    v_dim = n_v * GDN_HEAD_DIM
    assert n_ssm_heads + 2 * n_v == SMALL_W and seq % SEQ_TILE == 0 and seq % SSD_TILE == 0

    sizes = (d_in, d_in + 2 * gn, n_ssm_heads, 2 * qk_dim + v_dim, v_dim, n_v, n_v, d, d)
    starts = [0]
    for s in sizes:
        starts.append(starts[-1] + s)
    (s_z, s_xbc, s_dt, s_qkv, s_gz, s_beta, s_a, s_gs, s_gg) = starts[:-1]
    offs = {"ssm_z": 0, "ssm_x": d_in, "ssm_b": 2 * d_in, "ssm_c": 2 * d_in + gn}
    o_qkv = 2 * d_in + 2 * gn
    offs.update({"gdn_q": o_qkv, "gdn_k": o_qkv + qk_dim, "gdn_v": o_qkv + 2 * qk_dim})
    offs["gdn_z"] = o_qkv + 2 * qk_dim + v_dim
    offs["gate_ssm"] = offs["gdn_z"] + v_dim
    offs["gate_gdn"] = offs["gate_ssm"] + d

    vb = GDN_QK_PER_STEP * GDN_V_PER_QK
    e_dt = _selector(SSM_N_GROUPS, 0, n_ssm_heads // SSM_N_GROUPS, d_in // SSM_N_GROUPS, SSM_HEAD_DIM)
    e_beta = _selector(n_v // vb, n_ssm_heads, vb, vb * GDN_HEAD_DIM, GDN_HEAD_DIM)
    e_gam = _selector(n_v // vb, n_ssm_heads + n_v, vb, vb * GDN_HEAD_DIM, GDN_HEAD_DIM)

    out = x
    for layer in range(depth):
        wl = w_in[layer]
        w_small = jnp.concatenate([wl[:, s_dt:s_qkv], wl[:, s_beta:s_gs]], axis=1)

        mod = _adaln(c, w_ada[layer], b_ada[layer])
        mod3 = mod.reshape(bsz, 1, 6 * d)
        h = _norm_mod(out, norm_mix_w[layer], mod3, 0, 1).reshape(bsz * seq, d)
        proj = _inproj(h, w_in, layer, offs, offs["gate_gdn"] + d, s_qkv - s_dt, s_gs - s_gz - v_dim + s_qkv - s_dt)
        small = _matmul(h, w_small, 1024, SMALL_W, F32, "in_proj_small")

        zeros_b = jnp.zeros((n_v,), F32)
        bias = jnp.concatenate([ssm_dt_bias[layer], zeros_b, gdn_dt_bias[layer]])[None, :]
        alog = jnp.concatenate([ssm_a_log[layer], zeros_b, gdn_a_log[layer]])[None, :]
        p1, p2, p2t = _prep(small, bias, alog, n_ssm_heads, n_v)

        y_ssm = _ssd(proj, p1, p2, p2t, e_dt, ssm_conv_w[layer], ssm_conv_b[layer], ssm_d_skip[layer],
                     ssm_norm_w[layer], bsz, seq, offs)
        y_gdn = _gdn(proj, p1, p2, p2t, e_beta, e_gam, gdn_conv_w[layer], gdn_norm_w[layer][None, :],
                     bsz, seq, offs, n_qk, n_ssm_heads + n_v)

        merged = _merge(y_ssm, y_gdn, proj, w_ssm_proj[layer].astype(BF16), w_gdn_proj[layer].astype(BF16), offs)
        x1, h2 = _mixout(merged, w_o[layer].astype(BF16), out, mod3, norm_ffn_w[layer], 2, 3, 4)
        act = _gateup(h2.reshape(bsz * seq, d), w_gate_up, layer)
        out = _down(act, w_down[layer].astype(BF16), x1, mod3, final_norm_w, 5, layer == depth - 1)
    return out
```

```python
import functools

import jax
import jax.numpy as jnp
from jax import lax
from jax.experimental import pallas as pl
from jax.experimental.pallas import tpu as pltpu

F32 = jnp.float32
BF16 = jnp.bfloat16

NORM_EPS = 1e-6
L2_EPS = 1e-6
CONV_K = 4
SUBLANES = 8
HALO = 8
CHUNK = 64
INV_BASE = 16
SEQ_TILE = 256
SSD_TILE = 1024
SSM_HEAD_DIM = 64
SSM_N_GROUPS = 8
SSM_D_STATE = 128
GDN_HEAD_DIM = 128
GDN_V_PER_QK = 2
GDN_QK_PER_STEP = 8
SMALL_W = 128
INPROJ_ROWS = 256
VMEM_LIMIT = 56 * 1024 * 1024


def _cparams(sem):
    return pltpu.CompilerParams(dimension_semantics=sem, vmem_limit_bytes=VMEM_LIMIT)


def _bdot(a, b):
    return jnp.dot(a.astype(BF16), b.astype(BF16), preferred_element_type=F32)


def _bdot_nt(a, b):
    return lax.dot_general(a.astype(BF16), b.astype(BF16), (((1,), (1,)), ((), ())),
                           preferred_element_type=F32)


def _bdot_tn(a, b):
    return lax.dot_general(a.astype(BF16), b.astype(BF16), (((0,), (0,)), ((), ())),
                           preferred_element_type=F32)


def _split3(x):
    hi = x.astype(BF16)
    r1 = x - hi.astype(F32)
    mid = r1.astype(BF16)
    lo = (r1 - mid.astype(F32)).astype(BF16)
    return hi, mid, lo


def _dot01_left(m01, x):
    hi, mid, lo = _split3(x)
    d = lambda p: jnp.dot(m01, p, preferred_element_type=F32)
    return (d(lo) + d(mid)) + d(hi)


def _dot01_right(x, m01):
    hi, mid, lo = _split3(x)
    d = lambda p: jnp.dot(p, m01, preferred_element_type=F32)
    return (d(lo) + d(mid)) + d(hi)


def _dot01_tn(x, m01):
    hi, mid, lo = _split3(x)
    d = lambda p: lax.dot_general(p, m01, (((0,), (0,)), ((), ())), preferred_element_type=F32)
    return (d(lo) + d(mid)) + d(hi)


def _sigmoid(x):
    return 1.0 / (1.0 + jnp.exp(-x))


def _silu(x):
    return x * _sigmoid(x)


def _silu_tanh(x):
    h = 0.5 * x
    return h * jnp.tanh(h) + h


def _softplus(x):
    return jnp.maximum(x, 0.0) + jnp.log(1.0 + jnp.exp(-jnp.abs(x)))


def _rms(x, eps=NORM_EPS):
    return x * lax.rsqrt(jnp.mean(x * x, axis=-1, keepdims=True) + eps)


def _conv_silu(stage_ref, cur, w_ref, bias, rows):
    stage_ref[HALO:HALO + rows, :] = cur
    acc = None
    for k in reversed(range(CONV_K)):
        start = HALO - (CONV_K - 1) + k
        term = stage_ref[start:start + rows, :] * w_ref[k:k + 1, :]
        acc = term if acc is None else acc + term
    if bias is not None:
        acc = acc + bias
    stage_ref[0:HALO, :] = cur[rows - HALO:rows, :]
    return _silu_tanh(acc)


def _ada_kernel(c_ref, w_ref, b_ref, o_ref):
    c_act = _silu(c_ref[...])
    o_ref[...] = _bdot(c_act, w_ref[...]) + b_ref[...]


def _adaln(c, w_ada, b_ada):
    bsz, d = c.shape
    n = w_ada.shape[1]
    tn = 1024
    return pl.pallas_call(
        _ada_kernel,
        grid=(n // tn,),
        in_specs=[pl.BlockSpec((bsz, d), lambda j: (0, 0)),
                  pl.BlockSpec((d, tn), lambda j: (0, j)),
                  pl.BlockSpec((1, tn), lambda j: (0, j))],
        out_specs=pl.BlockSpec((bsz, tn), lambda j: (0, j)),
        out_shape=jax.ShapeDtypeStruct((bsz, n), F32),
        compiler_params=_cparams(("arbitrary",)),
        name="adaln",
    )(c, w_ada, b_ada.reshape(1, n))


def _norm_mod_kernel(x_ref, w_ref, shift_ref, scale_ref, o_ref):
    y = _rms(x_ref[...]) * w_ref[...]
    o_ref[...] = (y * (1.0 + scale_ref[...]) + shift_ref[...]).astype(o_ref.dtype)


def _norm_mod(x, w, mod3, shift_idx, scale_idx):
    bsz, seq, d = x.shape
    tm = 512
    return pl.pallas_call(
        _norm_mod_kernel,
        grid=(bsz, seq // tm),
        in_specs=[pl.BlockSpec((None, tm, d), lambda b, i: (b, i, 0)),
                  pl.BlockSpec((1, d), lambda b, i: (0, 0)),
                  pl.BlockSpec((None, 1, d), lambda b, i: (b, 0, shift_idx)),
                  pl.BlockSpec((None, 1, d), lambda b, i: (b, 0, scale_idx))],
        out_specs=pl.BlockSpec((None, tm, d), lambda b, i: (b, i, 0)),
        out_shape=jax.ShapeDtypeStruct((bsz, seq, d), BF16),
        compiler_params=_cparams(("arbitrary", "arbitrary")),
        name="norm_mod",
    )(x, w.reshape(1, d), mod3, mod3)


def _mm_kernel(x_ref, w_ref, o_ref):
    w = w_ref[...].astype(x_ref.dtype)
    o_ref[...] = jnp.dot(x_ref[...], w, preferred_element_type=F32).astype(o_ref.dtype)


def _matmul(x, w, tm, tn, out_dtype, name):
    m, k = x.shape
    n = w.shape[1]
    return pl.pallas_call(
        _mm_kernel,
        grid=(n // tn, m // tm),
        in_specs=[pl.BlockSpec((tm, k), lambda j, i: (i, 0)),
                  pl.BlockSpec((k, tn), lambda j, i: (0, j))],
        out_specs=pl.BlockSpec((tm, tn), lambda j, i: (i, j)),
        out_shape=jax.ShapeDtypeStruct((m, n), out_dtype),
        compiler_params=_cparams(("arbitrary", "arbitrary")),
        name=name,
    )(x, w)


def _inproj_kernel(x_ref, wa_ref, wb_ref, o_ref, wbf_ref, *, n_shift1, n_shift2, shift1, shift2):
    n = pl.program_id(0)
    kdim = x_ref.shape[1]

    @pl.when(pl.program_id(1) == 0)
    def _():
        def repack(shift):
            for r0 in range(0, kdim, INPROJ_ROWS):
                rs = slice(r0, r0 + INPROJ_ROWS)
                if shift == 0:
                    w = wa_ref[rs, :]
                else:
                    w = jnp.concatenate([wa_ref[rs, shift:], wb_ref[rs, :shift]], axis=1)
                wbf_ref[rs, :] = w.astype(BF16)

        @pl.when(n < n_shift1)
        def _():
            repack(0)

        @pl.when((n >= n_shift1) & (n < n_shift2))
        def _():
            repack(shift1)

        @pl.when(n >= n_shift2)
        def _():
            repack(shift2)

    o_ref[...] = jnp.dot(x_ref[...], wbf_ref[...], preferred_element_type=F32)


def _inproj(h, w_in, layer, offs, n_main, shift1, shift2):
    m, k = h.shape
    tm, tn = 1024, 1024
    segs = (offs["gate_ssm"], n_main, offs["gdn_q"])
    assert all(s % tn == 0 for s in segs) and shift2 == SMALL_W and 0 < shift1 < SMALL_W
    return pl.pallas_call(
        functools.partial(_inproj_kernel, n_shift1=offs["gdn_q"] // tn, n_shift2=offs["gate_ssm"] // tn,
                          shift1=shift1, shift2=shift2),
        grid=(n_main // tn, m // tm),
        in_specs=[pl.BlockSpec((tm, k), lambda j, i: (i, 0)),
                  pl.BlockSpec((None, k, tn), lambda j, i: (layer, 0, j)),
                  pl.BlockSpec((None, k, SMALL_W), lambda j, i: (layer, 0, (j + 1) * (tn // SMALL_W)))],
        out_specs=pl.BlockSpec((tm, tn), lambda j, i: (i, j)),
        out_shape=jax.ShapeDtypeStruct((m, n_main), F32),
        scratch_shapes=[pltpu.VMEM((k, tn), BF16)],
        compiler_params=_cparams(("arbitrary", "arbitrary")),
        name="in_proj",
    )(h, w_in, w_in)


def _prep_kernel(sm_ref, bias_ref, alog_ref, tril_ref, eye_ref, p1_ref, p2_ref, p2t_ref, *, n_dt, n_beta):
    sm = sm_ref[...]
    lane = lax.broadcasted_iota(jnp.int32, sm.shape, 1)
    is_dt = lane < n_dt
    is_beta = (lane >= n_dt) & (lane < n_dt + n_beta)
    sp = _softplus(sm + bias_ref[...])
    rate = -jnp.exp(alog_ref[...]) * sp
    p1_ref[...] = jnp.where(is_beta, _sigmoid(sm), jnp.where(is_dt, sp, rate))
    cum = _dot01_left(tril_ref[...], jnp.where(is_beta, 0.0, rate))
    p2_ref[...] = cum
    p2t_ref[...] = _dot01_tn(cum, eye_ref[...])


def _prep(small, bias, alog, n_dt, n_beta):
    t = small.shape[0]
    ts = SEQ_TILE
    r = jnp.arange(ts)
    tril = ((r[:, None] // CHUNK == r[None, :] // CHUNK) & (r[:, None] >= r[None, :])).astype(BF16)
    eye = (r[:, None] == r[None, :]).astype(BF16)
    return pl.pallas_call(
        functools.partial(_prep_kernel, n_dt=n_dt, n_beta=n_beta),
        grid=(t // ts,),
        in_specs=[pl.BlockSpec((ts, SMALL_W), lambda i: (i, 0)),
                  pl.BlockSpec((1, SMALL_W), lambda i: (0, 0)),
                  pl.BlockSpec((1, SMALL_W), lambda i: (0, 0)),
                  pl.BlockSpec((ts, ts), lambda i: (0, 0)),
                  pl.BlockSpec((ts, ts), lambda i: (0, 0))],
        out_specs=[pl.BlockSpec((ts, SMALL_W), lambda i: (i, 0)),
                   pl.BlockSpec((ts, SMALL_W), lambda i: (i, 0)),
                   pl.BlockSpec((SMALL_W, ts), lambda i: (0, i))],
        out_shape=[jax.ShapeDtypeStruct((t, SMALL_W), F32),
                   jax.ShapeDtypeStruct((t, SMALL_W), F32),
                   jax.ShapeDtypeStruct((SMALL_W, t), F32)],
        compiler_params=_cparams(("arbitrary",)),
        name="head_scalars",
    )(small, bias, alog, tril, eye)


def _ssd_kernel(z_ref, x_ref, b_ref, c_ref, p1_ref, p2_ref, arow_ref, e_ref, causal_ref, headbd_ref,
                xpar_ref, bpar_ref, cpar_ref, o_ref,
                xs_ref, bs_ref, cs_ref, st_ref):
    rows = x_ref.shape[0]
    gw = x_ref.shape[1]
    n_heads = gw // SSM_HEAD_DIM
    bias_row = slice(CONV_K, CONV_K + 1)
    dsk_row = slice(CONV_K + 1, CONV_K + 2)
    nw_row = slice(CONV_K + 2, CONV_K + 3)

    @pl.when(pl.program_id(2) == 0)
    def _():
        xs_ref[0:HALO, :] = jnp.zeros((HALO, gw), F32)
        bs_ref[0:HALO, :] = jnp.zeros((HALO, SSM_D_STATE), F32)
        cs_ref[0:HALO, :] = jnp.zeros((HALO, SSM_D_STATE), F32)
        st_ref[...] = jnp.zeros_like(st_ref)

    xa = _conv_silu(xs_ref, x_ref[...], xpar_ref, xpar_ref[bias_row, :], rows)
    ba = _conv_silu(bs_ref, b_ref[...], bpar_ref, bpar_ref[bias_row, :], rows)
    ca = _conv_silu(cs_ref, c_ref[...], cpar_ref, cpar_ref[bias_row, :], rows)

    e = e_ref[...]
    dt = _dot01_right(p1_ref[...], e)
    a_cum = _dot01_right(p2_ref[...], e)
    xdt = xa * dt
    ea = jnp.exp(a_cum)
    head_bd = headbd_ref[...]

    ys = []
    for ci in range(rows // CHUNK):
        sl = slice(ci * CHUNK, (ci + 1) * CHUNK)
        b_c, c_c = ba[sl], ca[sl]
        ac = a_cum[sl]
        a_last = ac[CHUNK - 1:CHUNK, :]
        xdt_c = xdt[sl]
        a_row = jnp.concatenate([arow_ref[h:h + 1, sl] for h in range(n_heads)], axis=1)
        decay = jnp.exp(jnp.minimum(ac - a_row, 0.0)) * causal_ref[...]
        cb = _bdot_nt(c_c, jnp.concatenate([b_c] * n_heads, axis=0))
        scores = cb * decay
        x_bd = jnp.concatenate([xdt_c.astype(BF16)] * n_heads, axis=0) * head_bd
        y_diag = jnp.dot(scores.astype(BF16), x_bd, preferred_element_type=F32)
        state = st_ref[...]
        y_off = _bdot(c_c, state) * ea[sl]
        xw = xdt_c * jnp.exp(a_last - ac)
        st_ref[...] = state * jnp.exp(a_last) + _bdot_tn(b_c, xw)
        ys.append(y_diag + y_off)
    y = jnp.concatenate(ys, axis=0) + xpar_ref[dsk_row, :] * xa
    y = y * _silu_tanh(z_ref[...])
    o_ref[...] = (_rms(y) * xpar_ref[nw_row, :]).astype(o_ref.dtype)


def _ssd(proj, p1, p2, p2t, e_sel, conv_w, conv_b, d_skip, norm_w, bsz, seq, offs):
    d_in = norm_w.shape[0]
    gw = d_in // SSM_N_GROUPS
    hpg = gw // SSM_HEAD_DIM
    n = SSM_D_STATE
    ts = SSD_TILE
    nt = seq // ts
    zrow = jnp.zeros_like(conv_b)
    xtra = lambda v: jnp.concatenate([v, jnp.zeros((conv_b.shape[0] - d_in,), F32)])
    par = jnp.concatenate([conv_w, conv_b[None, :], xtra(jnp.repeat(d_skip, SSM_HEAD_DIM))[None, :],
                           xtra(norm_w)[None, :], zrow[None, :]], axis=0)
    assert par.shape[0] == SUBLANES
    z0 = offs["ssm_z"] // gw
    x0 = offs["ssm_x"] // gw
    b0 = offs["ssm_b"] // n
    c0 = offs["ssm_c"] // n
    cb0 = d_in // n
    cc0 = (d_in + SSM_N_GROUPS * n) // n
    row = lambda b, g, i: b * nt + i
    r = jnp.arange(CHUNK)[:, None]
    l = jnp.arange(gw)[None, :]
    causal = (r >= l % SSM_HEAD_DIM).astype(F32)
    k = jnp.arange(gw)
    head_bd = (k[:, None] // SSM_HEAD_DIM == k[None, :] // SSM_HEAD_DIM).astype(BF16)
    return pl.pallas_call(
        _ssd_kernel,
        grid=(bsz, SSM_N_GROUPS, nt),
        in_specs=[
            pl.BlockSpec((ts, gw), lambda b, g, i: (row(b, g, i), z0 + g)),
            pl.BlockSpec((ts, gw), lambda b, g, i: (row(b, g, i), x0 + g)),
            pl.BlockSpec((ts, n), lambda b, g, i: (row(b, g, i), b0 + g)),
            pl.BlockSpec((ts, n), lambda b, g, i: (row(b, g, i), c0 + g)),
            pl.BlockSpec((ts, SMALL_W), lambda b, g, i: (row(b, g, i), 0)),
            pl.BlockSpec((ts, SMALL_W), lambda b, g, i: (row(b, g, i), 0)),
            pl.BlockSpec((hpg, ts), lambda b, g, i: (g, row(b, g, i))),
            pl.BlockSpec((None, SMALL_W, gw), lambda b, g, i: (g, 0, 0)),
            pl.BlockSpec((CHUNK, gw), lambda b, g, i: (0, 0)),
            pl.BlockSpec((gw, gw), lambda b, g, i: (0, 0)),
            pl.BlockSpec((SUBLANES, gw), lambda b, g, i: (0, g)),
            pl.BlockSpec((SUBLANES, n), lambda b, g, i: (0, cb0 + g)),
            pl.BlockSpec((SUBLANES, n), lambda b, g, i: (0, cc0 + g)),
        ],
        out_specs=pl.BlockSpec((ts, gw), lambda b, g, i: (row(b, g, i), g)),
        out_shape=jax.ShapeDtypeStruct((bsz * seq, d_in), BF16),
        scratch_shapes=[pltpu.VMEM((ts + HALO, gw), F32),
                        pltpu.VMEM((ts + HALO, n), F32),
                        pltpu.VMEM((ts + HALO, n), F32),
                        pltpu.VMEM((n, gw), F32)],
        compiler_params=_cparams(("arbitrary", "arbitrary", "arbitrary")),
        name="ssd_mixer",
    )(proj, proj, proj, proj, p1, p2, p2t, e_sel, causal, head_bd, par, par, par)


def _gdn_kernel(q_ref, k_ref, v_ref, z_ref, p1_ref, p2_ref, grow_ref, eb_ref, eg_ref,
                incl_ref, eye_ref, nbase_ref, e1_ref, e2_ref,
                cwq_ref, cwk_ref, cwv_ref, nw_ref, o_ref,
                qs_ref, ks_ref, vs_ref, st_ref):
    rows = q_ref.shape[0]
    dk = dv = GDN_HEAD_DIM
    n_qk = q_ref.shape[1] // dk
    n_v = v_ref.shape[1] // dv
    assert rows == 2 * dk

    @pl.when(pl.program_id(2) == 0)
    def _():
        qs_ref[0:HALO, :] = jnp.zeros((HALO, n_qk * dk), F32)
        ks_ref[0:HALO, :] = jnp.zeros((HALO, n_qk * dk), F32)
        vs_ref[0:HALO, :] = jnp.zeros((HALO, n_v * dv), F32)
        st_ref[...] = jnp.zeros_like(st_ref)

    q_all = _conv_silu(qs_ref, q_ref[...], cwq_ref, None, rows)
    k_all = _conv_silu(ks_ref, k_ref[...], cwk_ref, None, rows)
    v_all = _conv_silu(vs_ref, v_ref[...], cwv_ref, None, rows)
    beta = _dot01_right(p1_ref[...], eb_ref[...])
    gam = _dot01_right(p2_ref[...], eg_ref[...])
    e_gam = jnp.exp(gam)
    incl_bf = incl_ref[...]
    g_off = (pl.program_id(1) * n_v) % SUBLANES

    qs, ks, kks, qks = [], [], [], []
    for j in range(n_qk):
        js = slice(j * dk, (j + 1) * dk)
        q = q_all[:, js]
        k = k_all[:, js]
        q = q * (lax.rsqrt(jnp.sum(q * q, axis=-1, keepdims=True) + L2_EPS) * (dk ** -0.5))
        k = k * lax.rsqrt(jnp.sum(k * k, axis=-1, keepdims=True) + L2_EPS)
        qs.append(q)
        ks.append(k)
        kks.append(_bdot_nt(k, k))
        qks.append(_bdot_nt(q, k))

    heads = range(n_v)
    eye = eye_ref[...]
    t_inv, sq, e_lo, rhs, qkm, gbs = [], [], [], [], [], []
    for h in heads:
        j = h // GDN_V_PER_QK
        hs = slice(h * dv, (h + 1) * dv)
        gb = gam[:, hs]
        bb = beta[:, hs]
        seg = jnp.concatenate([gb, gb], axis=1) - grow_ref[pl.ds(g_off + h, 1), :]
        dec = jnp.exp(jnp.minimum(seg, 0.0))
        a = kks[j] * dec * jnp.concatenate([bb, bb], axis=1)
        b0 = a * nbase_ref[...]
        sq.append(b0)
        t_inv.append(eye + b0)
        a_bf = a.astype(BF16)
        e_lo.append([a_bf * m_ref[...] for m_ref in (e1_ref, e2_ref)])
        rhs.append(jnp.concatenate([v_all[:, hs] * bb, ks[j] * (bb * e_gam[:, hs])], axis=1))
        qkm.append((qks[j] * dec).astype(BF16) * incl_bf)
        gbs.append(gb)

    n_sq = INV_BASE.bit_length() - 2
    for h in heads:
        sq[h] = _bdot(sq[h], sq[h])
    for lvl in range(n_sq):
        for h in heads:
            if lvl == n_sq - 1:
                t_inv[h] = t_inv[h] + _bdot(t_inv[h], sq[h])
            else:
                res = _bdot(jnp.concatenate([t_inv[h], sq[h]], axis=0), sq[h])
                t_inv[h] = t_inv[h] + res[:rows]
                sq[h] = res[rows:]
    for lvl in range(len(e_lo[0])):
        for h in heads:
            t_inv[h] = t_inv[h] - _bdot(t_inv[h], _bdot(e_lo[h][lvl], t_inv[h]))
    xs = [_bdot(t_inv[h], rhs[h]) for h in heads]

    qd = [qs[h // GDN_V_PER_QK] * e_gam[:, h * dv:(h + 1) * dv] for h in heads]
    qs_parts = [[] for _ in heads]
    vn_parts = [[] for _ in heads]
    for ci in range(rows // CHUNK):
        sl = slice(ci * CHUNK, (ci + 1) * CHUNK)
        for h in heads:
            j = h // GDN_V_PER_QK
            u_c, w_c = xs[h][sl, :dv], xs[h][sl, dv:]
            gb = gbs[h]
            gl = gb[ci * CHUNK + CHUNK - 1:(ci + 1) * CHUNK, :]
            state = st_ref[h]
            wq = _bdot(jnp.concatenate([w_c, qd[h][sl]], axis=0), state)
            v_new = u_c - wq[:CHUNK]
            k_tail = ks[j][sl] * jnp.exp(gl - gb[sl])
            st_ref[h] = state * jnp.exp(gl) + _bdot_tn(k_tail, v_new)
            qs_parts[h].append(wq[CHUNK:])
            vn_parts[h].append(v_new)
    z = z_ref[...]
    for h in heads:
        hs = slice(h * dv, (h + 1) * dv)
        o = jnp.concatenate(qs_parts[h], axis=0) + jnp.dot(
            qkm[h], jnp.concatenate(vn_parts[h], axis=0).astype(BF16), preferred_element_type=F32)
        o = _rms(o) * nw_ref[...] * _silu_tanh(z[:, hs])
        o_ref[:, hs] = o.astype(o_ref.dtype)


def _gdn(proj, p1, p2, p2t, e_beta, e_gam, conv_w, norm_w, bsz, seq, offs, n_qk, g_row0):
    dk = GDN_HEAD_DIM
    qb = GDN_QK_PER_STEP
    vb = qb * GDN_V_PER_QK
    qw = qb * dk
    vw = vb * GDN_HEAD_DIM
    ts = SEQ_TILE
    nt = seq // ts
    q0 = offs["gdn_q"] // qw
    k0 = offs["gdn_k"] // qw
    v0 = offs["gdn_v"] // vw
    z0 = offs["gdn_z"] // vw
    ck0 = (n_qk * dk) // qw
    cv0 = (2 * n_qk * dk) // vw
    assert g_row0 % SUBLANES == 0 and (SUBLANES % vb == 0 or vb % SUBLANES == 0)
    row = lambda b, g, i: b * nt + i
    r = jnp.arange(ts)
    same = r[:, None] // CHUNK == r[None, :] // CHUNK
    lower = r[:, None] > r[None, :]
    incl = (same & (r[:, None] >= r[None, :])).astype(BF16)
    eye = (r[:, None] == r[None, :]).astype(F32)
    blk = lambda size: r[:, None] // size == r[None, :] // size
    nbase = -(blk(INV_BASE) & lower).astype(F32)
    e_masks = []
    size = INV_BASE
    while size < CHUNK:
        e_masks.append((blk(2 * size) & ~blk(size) & lower).astype(BF16))
        size *= 2
    assert len(e_masks) == 2
    return pl.pallas_call(
        _gdn_kernel,
        grid=(bsz, n_qk // qb, nt),
        in_specs=[
            pl.BlockSpec((ts, qw), lambda b, g, i: (row(b, g, i), q0 + g)),
            pl.BlockSpec((ts, qw), lambda b, g, i: (row(b, g, i), k0 + g)),
            pl.BlockSpec((ts, vw), lambda b, g, i: (row(b, g, i), v0 + g)),
            pl.BlockSpec((ts, vw), lambda b, g, i: (row(b, g, i), z0 + g)),
            pl.BlockSpec((ts, SMALL_W), lambda b, g, i: (row(b, g, i), 0)),
            pl.BlockSpec((ts, SMALL_W), lambda b, g, i: (row(b, g, i), 0)),
            pl.BlockSpec((max(vb, SUBLANES), ts),
                         lambda b, g, i: ((g_row0 + g * vb) // max(vb, SUBLANES), row(b, g, i))),
            pl.BlockSpec((None, SMALL_W, vw), lambda b, g, i: (g, 0, 0)),
            pl.BlockSpec((None, SMALL_W, vw), lambda b, g, i: (g, 0, 0)),
            pl.BlockSpec((ts, ts), lambda b, g, i: (0, 0)),
            pl.BlockSpec((ts, ts), lambda b, g, i: (0, 0)),
            pl.BlockSpec((ts, ts), lambda b, g, i: (0, 0)),
            pl.BlockSpec((ts, ts), lambda b, g, i: (0, 0)),
            pl.BlockSpec((ts, ts), lambda b, g, i: (0, 0)),
            pl.BlockSpec((CONV_K, qw), lambda b, g, i: (0, g)),
            pl.BlockSpec((CONV_K, qw), lambda b, g, i: (0, ck0 + g)),
            pl.BlockSpec((CONV_K, vw), lambda b, g, i: (0, cv0 + g)),
            pl.BlockSpec((1, dk), lambda b, g, i: (0, 0)),
        ],
        out_specs=pl.BlockSpec((ts, vw), lambda b, g, i: (row(b, g, i), g)),
        out_shape=jax.ShapeDtypeStruct((bsz * seq, n_qk * GDN_V_PER_QK * GDN_HEAD_DIM), BF16),
        scratch_shapes=[pltpu.VMEM((ts + HALO, qw), F32),
                        pltpu.VMEM((ts + HALO, qw), F32),
                        pltpu.VMEM((ts + HALO, vw), F32),
                        pltpu.VMEM((vb, dk, GDN_HEAD_DIM), F32)],
        compiler_params=_cparams(("arbitrary", "arbitrary", "arbitrary")),
        name="gdn_mixer",
    )(proj, proj, proj, proj, p1, p2, p2t, e_beta, e_gam, incl, eye, nbase, e_masks[0], e_masks[1],
      conv_w, conv_w, conv_w, norm_w)


def _merge_kernel(ys_ref, yg_ref, gs_ref, gg_ref, ws_ref, wg_ref, o_ref):
    a = jnp.dot(ys_ref[...], ws_ref[...], preferred_element_type=F32)
    b = jnp.dot(yg_ref[...], wg_ref[...], preferred_element_type=F32)
    o_ref[...] = (_sigmoid(gs_ref[...]) * a + _sigmoid(gg_ref[...]) * b).astype(o_ref.dtype)


def _merge(y_ssm, y_gdn, proj, w_ssm, w_gdn, offs):
    m, k = y_ssm.shape
    n = w_ssm.shape[1]
    tm, tn = 512, 512
    gs0 = offs["gate_ssm"] // tn
    gg0 = offs["gate_gdn"] // tn
    return pl.pallas_call(
        _merge_kernel,
        grid=(n // tn, m // tm),
        in_specs=[pl.BlockSpec((tm, k), lambda j, i: (i, 0)),
                  pl.BlockSpec((tm, k), lambda j, i: (i, 0)),
                  pl.BlockSpec((tm, tn), lambda j, i: (i, gs0 + j)),
                  pl.BlockSpec((tm, tn), lambda j, i: (i, gg0 + j)),
                  pl.BlockSpec((k, tn), lambda j, i: (0, j)),
                  pl.BlockSpec((k, tn), lambda j, i: (0, j))],
        out_specs=pl.BlockSpec((tm, tn), lambda j, i: (i, j)),
        out_shape=jax.ShapeDtypeStruct((m, n), BF16),
        compiler_params=_cparams(("arbitrary", "arbitrary")),
        name="merge_proj",
    )(y_ssm, y_gdn, proj, proj, w_ssm, w_gdn)


def _mixout_kernel(m_ref, wo_ref, x_ref, gate_ref, nw_ref, shift_ref, scale_ref, x1_ref, h2_ref):
    mix = jnp.dot(m_ref[...], wo_ref[...], preferred_element_type=F32)
    x1 = x_ref[...] + gate_ref[...] * mix
    x1_ref[...] = x1
    y = _rms(x1) * nw_ref[...]
    h2_ref[...] = (y * (1.0 + scale_ref[...]) + shift_ref[...]).astype(h2_ref.dtype)


def _mixout(merged, w_o, x, mod3, norm_w, gate_idx, shift_idx, scale_idx):
    bsz, seq, d = x.shape
    tm = 512
    nt = seq // tm
    return pl.pallas_call(
        _mixout_kernel,
        grid=(bsz, nt),
        in_specs=[pl.BlockSpec((tm, d), lambda b, i: (b * nt + i, 0)),
                  pl.BlockSpec((d, d), lambda b, i: (0, 0)),
                  pl.BlockSpec((None, tm, d), lambda b, i: (b, i, 0)),
                  pl.BlockSpec((None, 1, d), lambda b, i: (b, 0, gate_idx)),
                  pl.BlockSpec((1, d), lambda b, i: (0, 0)),
                  pl.BlockSpec((None, 1, d), lambda b, i: (b, 0, shift_idx)),
                  pl.BlockSpec((None, 1, d), lambda b, i: (b, 0, scale_idx))],
        out_specs=[pl.BlockSpec((None, tm, d), lambda b, i: (b, i, 0)),
                   pl.BlockSpec((None, tm, d), lambda b, i: (b, i, 0))],
        out_shape=[jax.ShapeDtypeStruct((bsz, seq, d), F32),
                   jax.ShapeDtypeStruct((bsz, seq, d), BF16)],
        compiler_params=_cparams(("arbitrary", "arbitrary")),
        name="mix_out",
    )(merged, w_o, x, mod3, norm_w.reshape(1, d), mod3, mod3)


def _gateup_kernel(h_ref, wg_ref, wu_ref, o_ref, wgb_ref, wub_ref):
    @pl.when(pl.program_id(1) == 0)
    def _():
        wgb_ref[...] = wg_ref[...].astype(BF16)
        wub_ref[...] = wu_ref[...].astype(BF16)

    h = h_ref[...]
    g = jnp.dot(h, wgb_ref[...], preferred_element_type=F32)
    u = jnp.dot(h, wub_ref[...], preferred_element_type=F32)
    o_ref[...] = (_silu(g) * u).astype(o_ref.dtype)


def _gateup(h, w_gate_up, layer):
    m, k = h.shape
    hid = w_gate_up.shape[2] // 2
    tm, tn = 2048, 512
    up0 = hid // tn
    return pl.pallas_call(
        _gateup_kernel,
        grid=(hid // tn, m // tm),
        in_specs=[pl.BlockSpec((tm, k), lambda j, i: (i, 0)),
                  pl.BlockSpec((None, k, tn), lambda j, i: (layer, 0, j)),
                  pl.BlockSpec((None, k, tn), lambda j, i: (layer, 0, up0 + j))],
        out_specs=pl.BlockSpec((tm, tn), lambda j, i: (i, j)),
        out_shape=jax.ShapeDtypeStruct((m, hid), BF16),
        scratch_shapes=[pltpu.VMEM((k, tn), BF16), pltpu.VMEM((k, tn), BF16)],
        compiler_params=_cparams(("arbitrary", "arbitrary")),
        name="ffn_gate_up",
    )(h, w_gate_up, w_gate_up)


def _down_kernel(a_ref, wd_ref, x1_ref, gate_ref, fw_ref, o_ref, *, final):
    ffn = jnp.dot(a_ref[...], wd_ref[...], preferred_element_type=F32)
    x2 = x1_ref[...] + gate_ref[...] * ffn
    if final:
        x2 = _rms(x2) * fw_ref[...]
    o_ref[...] = x2


def _down(act, w_down, x1, mod3, final_w, gate_idx, final):
    bsz, seq, d = x1.shape
    hid = act.shape[1]
    tm = 512
    nt = seq // tm
    return pl.pallas_call(
        functools.partial(_down_kernel, final=final),
        grid=(bsz, nt),
        in_specs=[pl.BlockSpec((tm, hid), lambda b, i: (b * nt + i, 0)),
                  pl.BlockSpec((hid, d), lambda b, i: (0, 0), pipeline_mode=pl.Buffered(1)),
                  pl.BlockSpec((None, tm, d), lambda b, i: (b, i, 0)),
                  pl.BlockSpec((None, 1, d), lambda b, i: (b, 0, gate_idx)),
                  pl.BlockSpec((1, d), lambda b, i: (0, 0))],
        out_specs=pl.BlockSpec((None, tm, d), lambda b, i: (b, i, 0)),
        out_shape=jax.ShapeDtypeStruct((bsz, seq, d), F32),
        compiler_params=_cparams(("arbitrary", "arbitrary")),
        name="ffn_down",
    )(act, w_down, x1, mod3, final_w.reshape(1, d))


def _selector(n_groups, src0, per_group, width, rep):
    k = jnp.arange(SMALL_W)[None, :, None]
    g = jnp.arange(n_groups)[:, None, None]
    c = jnp.arange(width)[None, None, :]
    return (k == src0 + g * per_group + c // rep).astype(BF16)


def kernel(x, c, w_ada, b_ada, norm_mix_w, w_in, ssm_conv_w, ssm_conv_b, ssm_dt_bias, ssm_a_log, ssm_d_skip, ssm_norm_w, gdn_conv_w, gdn_a_log, gdn_dt_bias, gdn_norm_w, w_ssm_proj, w_gdn_proj, w_o, norm_ffn_w, w_gate_up, w_down, final_norm_w):
    bsz, seq, d = x.shape
    depth = w_ada.shape[0]
    d_in = ssm_norm_w.shape[1]
    n_ssm_heads = ssm_dt_bias.shape[1]
    gn = SSM_N_GROUPS * SSM_D_STATE
    n_v = gdn_a_log.shape[1]
    n_qk = n_v // GDN_V_PER_QK
    qk_dim = n_qk * GDN_HEAD_DIM
    v_dim = n_v * GDN_HEAD_DIM
    assert n_ssm_heads + 2 * n_v == SMALL_W and seq % SEQ_TILE == 0 and seq % SSD_TILE == 0

    sizes = (d_in, d_in + 2 * gn, n_ssm_heads, 2 * qk_dim + v_dim, v_dim, n_v, n_v, d, d)
    starts = [0]
    for s in sizes:
        starts.append(starts[-1] + s)
    (s_z, s_xbc, s_dt, s_qkv, s_gz, s_beta, s_a, s_gs, s_gg) = starts[:-1]
    offs = {"ssm_z": 0, "ssm_x": d_in, "ssm_b": 2 * d_in, "ssm_c": 2 * d_in + gn}
    o_qkv = 2 * d_in + 2 * gn
    offs.update({"gdn_q": o_qkv, "gdn_k": o_qkv + qk_dim, "gdn_v": o_qkv + 2 * qk_dim})
    offs["gdn_z"] = o_qkv + 2 * qk_dim + v_dim
    offs["gate_ssm"] = offs["gdn_z"] + v_dim
    offs["gate_gdn"] = offs["gate_ssm"] + d

    vb = GDN_QK_PER_STEP * GDN_V_PER_QK
    e_dt = _selector(SSM_N_GROUPS, 0, n_ssm_heads // SSM_N_GROUPS, d_in // SSM_N_GROUPS, SSM_HEAD_DIM)
    e_beta = _selector(n_v // vb, n_ssm_heads, vb, vb * GDN_HEAD_DIM, GDN_HEAD_DIM)
    e_gam = _selector(n_v // vb, n_ssm_heads + n_v, vb, vb * GDN_HEAD_DIM, GDN_HEAD_DIM)

    out = x
    for layer in range(depth):
        wl = w_in[layer]
        w_small = jnp.concatenate([wl[:, s_dt:s_qkv], wl[:, s_beta:s_gs]], axis=1)

        mod = _adaln(c, w_ada[layer], b_ada[layer])
        mod3 = mod.reshape(bsz, 1, 6 * d)
        h = _norm_mod(out, norm_mix_w[layer], mod3, 0, 1).reshape(bsz * seq, d)
        proj = _inproj(h, w_in, layer, offs, offs["gate_gdn"] + d, s_qkv - s_dt, s_gs - s_gz - v_dim + s_qkv - s_dt)
        small = _matmul(h, w_small, 1024, SMALL_W, F32, "in_proj_small")

        zeros_b = jnp.zeros((n_v,), F32)
        bias = jnp.concatenate([ssm_dt_bias[layer], zeros_b, gdn_dt_bias[layer]])[None, :]
        alog = jnp.concatenate([ssm_a_log[layer], zeros_b, gdn_a_log[layer]])[None, :]
        p1, p2, p2t = _prep(small, bias, alog, n_ssm_heads, n_v)

        y_ssm = _ssd(proj, p1, p2, p2t, e_dt, ssm_conv_w[layer], ssm_conv_b[layer], ssm_d_skip[layer],
                     ssm_norm_w[layer], bsz, seq, offs)
        y_gdn = _gdn(proj, p1, p2, p2t, e_beta, e_gam, gdn_conv_w[layer], gdn_norm_w[layer][None, :],
                     bsz, seq, offs, n_qk, n_ssm_heads + n_v)

        merged = _merge(y_ssm, y_gdn, proj, w_ssm_proj[layer].astype(BF16), w_gdn_proj[layer].astype(BF16), offs)
        x1, h2 = _mixout(merged, w_o[layer].astype(BF16), out, mod3, norm_ffn_w[layer], 2, 3, 4)
        act = _gateup(h2.reshape(bsz * seq, d), w_gate_up, layer)
        out = _down(act, w_down[layer].astype(BF16), x1, mod3, final_norm_w, 5, layer == depth - 1)
    return out
```

```python
import functools

import jax
import jax.numpy as jnp
from jax import lax
from jax.experimental import pallas as pl
from jax.experimental.pallas import tpu as pltpu

F32 = jnp.float32
BF16 = jnp.bfloat16

NORM_EPS = 1e-6
L2_EPS = 1e-6
CONV_K = 4
SUBLANES = 8
HALO = 8
CHUNK = 64
INV_BASE = 16
SEQ_TILE = 256
SSD_TILE = 1024
SSM_HEAD_DIM = 64
SSM_N_GROUPS = 8
SSM_D_STATE = 128
GDN_HEAD_DIM = 128
GDN_V_PER_QK = 2
GDN_QK_PER_STEP = 8
SMALL_W = 128
SPLIT_PARTS = 3
INPROJ_ROWS = 256
VMEM_LIMIT = 56 * 1024 * 1024


def _cparams(sem):
    return pltpu.CompilerParams(dimension_semantics=sem, vmem_limit_bytes=VMEM_LIMIT)


def _bdot(a, b):
    return jnp.dot(a.astype(BF16), b.astype(BF16), preferred_element_type=F32)


def _bdot_nt(a, b):
    return lax.dot_general(a.astype(BF16), b.astype(BF16), (((1,), (1,)), ((), ())),
                           preferred_element_type=F32)


def _bdot_tn(a, b):
    return lax.dot_general(a.astype(BF16), b.astype(BF16), (((0,), (0,)), ((), ())),
                           preferred_element_type=F32)


def _split3(x):
    hi = x.astype(BF16)
    r1 = x - hi.astype(F32)
    mid = r1.astype(BF16)
    lo = (r1 - mid.astype(F32)).astype(BF16)
    return hi, mid, lo


def _dot01_left(m01, x):
    hi, mid, lo = _split3(x)
    d = lambda p: jnp.dot(m01, p, preferred_element_type=F32)
    return (d(lo) + d(mid)) + d(hi)


def _dot01_right(x, m01_x3):
    return jnp.dot(jnp.concatenate(_split3(x), axis=1), m01_x3, preferred_element_type=F32)


def _dot01_tn(x, m01):
    hi, mid, lo = _split3(x)
    d = lambda p: lax.dot_general(p, m01, (((0,), (0,)), ((), ())), preferred_element_type=F32)
    return (d(lo) + d(mid)) + d(hi)


def _sigmoid(x):
    return 1.0 / (1.0 + jnp.exp(-x))


def _silu(x):
    return x * _sigmoid(x)


def _silu_tanh(x):
    h = 0.5 * x
    return h * jnp.tanh(h) + h


def _softplus(x):
    return jnp.maximum(x, 0.0) + jnp.log(1.0 + jnp.exp(-jnp.abs(x)))


def _rms(x, eps=NORM_EPS):
    return x * lax.rsqrt(jnp.mean(x * x, axis=-1, keepdims=True) + eps)


def _conv_silu(stage_ref, cur, w_ref, bias, rows):
    stage_ref[HALO:HALO + rows, :] = cur
    acc = None
    for k in reversed(range(CONV_K)):
        start = HALO - (CONV_K - 1) + k
        term = stage_ref[start:start + rows, :] * w_ref[k:k + 1, :]
        acc = term if acc is None else acc + term
    if bias is not None:
        acc = acc + bias
    stage_ref[0:HALO, :] = cur[rows - HALO:rows, :]
    return _silu_tanh(acc)


def _ada_kernel(c_ref, w_ref, b_ref, o_ref):
    c_act = _silu(c_ref[...])
    o_ref[...] = _bdot(c_act, w_ref[...]) + b_ref[...]


def _adaln(c, w_ada, b_ada):
    bsz, d = c.shape
    n = w_ada.shape[1]
    tn = 1024
    return pl.pallas_call(
        _ada_kernel,
        grid=(n // tn,),
        in_specs=[pl.BlockSpec((bsz, d), lambda j: (0, 0)),
                  pl.BlockSpec((d, tn), lambda j: (0, j)),
                  pl.BlockSpec((1, tn), lambda j: (0, j))],
        out_specs=pl.BlockSpec((bsz, tn), lambda j: (0, j)),
        out_shape=jax.ShapeDtypeStruct((bsz, n), F32),
        compiler_params=_cparams(("arbitrary",)),
        name="adaln",
    )(c, w_ada, b_ada.reshape(1, n))


def _norm_mod_kernel(x_ref, w_ref, shift_ref, scale_ref, o_ref):
    y = _rms(x_ref[...]) * w_ref[...]
    o_ref[...] = (y * (1.0 + scale_ref[...]) + shift_ref[...]).astype(o_ref.dtype)


def _norm_mod(x, w, mod3, shift_idx, scale_idx):
    bsz, seq, d = x.shape
    tm = 512
    return pl.pallas_call(
        _norm_mod_kernel,
        grid=(bsz, seq // tm),
        in_specs=[pl.BlockSpec((None, tm, d), lambda b, i: (b, i, 0)),
                  pl.BlockSpec((1, d), lambda b, i: (0, 0)),
                  pl.BlockSpec((None, 1, d), lambda b, i: (b, 0, shift_idx)),
                  pl.BlockSpec((None, 1, d), lambda b, i: (b, 0, scale_idx))],
        out_specs=pl.BlockSpec((None, tm, d), lambda b, i: (b, i, 0)),
        out_shape=jax.ShapeDtypeStruct((bsz, seq, d), BF16),
        compiler_params=_cparams(("arbitrary", "arbitrary")),
        name="norm_mod",
    )(x, w.reshape(1, d), mod3, mod3)


def _mm_kernel(x_ref, w_ref, o_ref):
    w = w_ref[...].astype(x_ref.dtype)
    o_ref[...] = jnp.dot(x_ref[...], w, preferred_element_type=F32).astype(o_ref.dtype)


def _matmul(x, w, tm, tn, out_dtype, name):
    m, k = x.shape
    n = w.shape[1]
    return pl.pallas_call(
        _mm_kernel,
        grid=(n // tn, m // tm),
        in_specs=[pl.BlockSpec((tm, k), lambda j, i: (i, 0)),
                  pl.BlockSpec((k, tn), lambda j, i: (0, j))],
        out_specs=pl.BlockSpec((tm, tn), lambda j, i: (i, j)),
        out_shape=jax.ShapeDtypeStruct((m, n), out_dtype),
        compiler_params=_cparams(("arbitrary", "arbitrary")),
        name=name,
    )(x, w)


def _inproj_kernel(x_ref, wa_ref, wb_ref, o_ref, wbf_ref, *, n_shift1, n_shift2, shift1, shift2):
    n = pl.program_id(0)
    kdim = x_ref.shape[1]

    @pl.when(pl.program_id(1) == 0)
    def _():
        def repack(shift):
            for r0 in range(0, kdim, INPROJ_ROWS):
                rs = slice(r0, r0 + INPROJ_ROWS)
                if shift == 0:
                    w = wa_ref[rs, :]
                else:
                    w = jnp.concatenate([wa_ref[rs, shift:], wb_ref[rs, :shift]], axis=1)
                wbf_ref[rs, :] = w.astype(BF16)

        @pl.when(n < n_shift1)
        def _():
            repack(0)

        @pl.when((n >= n_shift1) & (n < n_shift2))
        def _():
            repack(shift1)

        @pl.when(n >= n_shift2)
        def _():
            repack(shift2)

    o_ref[...] = jnp.dot(x_ref[...], wbf_ref[...], preferred_element_type=F32)


def _inproj(h, w_in, layer, offs, n_main, shift1, shift2):
    m, k = h.shape
    tm, tn = 1024, 1024
    segs = (offs["gate_ssm"], n_main, offs["gdn_q"])
    assert all(s % tn == 0 for s in segs) and shift2 == SMALL_W and 0 < shift1 < SMALL_W
    return pl.pallas_call(
        functools.partial(_inproj_kernel, n_shift1=offs["gdn_q"] // tn, n_shift2=offs["gate_ssm"] // tn,
                          shift1=shift1, shift2=shift2),
        grid=(n_main // tn, m // tm),
        in_specs=[pl.BlockSpec((tm, k), lambda j, i: (i, 0)),
                  pl.BlockSpec((None, k, tn), lambda j, i: (layer, 0, j)),
                  pl.BlockSpec((None, k, SMALL_W), lambda j, i: (layer, 0, (j + 1) * (tn // SMALL_W)))],
        out_specs=pl.BlockSpec((tm, tn), lambda j, i: (i, j)),
        out_shape=jax.ShapeDtypeStruct((m, n_main), F32),
        scratch_shapes=[pltpu.VMEM((k, tn), BF16)],
        compiler_params=_cparams(("arbitrary", "arbitrary")),
        name="in_proj",
    )(h, w_in, w_in)


def _prep_kernel(sm_ref, bias_ref, alog_ref, tril_ref, eye_ref, p1_ref, p2_ref, p2t_ref, *, n_dt, n_beta):
    sm = sm_ref[...]
    lane = lax.broadcasted_iota(jnp.int32, sm.shape, 1)
    is_dt = lane < n_dt
    is_beta = (lane >= n_dt) & (lane < n_dt + n_beta)
    sp = _softplus(sm + bias_ref[...])
    rate = -jnp.exp(alog_ref[...]) * sp
    p1_ref[...] = jnp.where(is_beta, _sigmoid(sm), jnp.where(is_dt, sp, rate))
    cum = _dot01_left(tril_ref[...], jnp.where(is_beta, 0.0, rate))
    p2_ref[...] = cum
    p2t_ref[...] = _dot01_tn(cum, eye_ref[...])


def _prep(small, bias, alog, n_dt, n_beta):
    t = small.shape[0]
    ts = SEQ_TILE
    r = jnp.arange(ts)
    tril = ((r[:, None] // CHUNK == r[None, :] // CHUNK) & (r[:, None] >= r[None, :])).astype(BF16)
    eye = (r[:, None] == r[None, :]).astype(BF16)
    return pl.pallas_call(
        functools.partial(_prep_kernel, n_dt=n_dt, n_beta=n_beta),
        grid=(t // ts,),
        in_specs=[pl.BlockSpec((ts, SMALL_W), lambda i: (i, 0)),
                  pl.BlockSpec((1, SMALL_W), lambda i: (0, 0)),
                  pl.BlockSpec((1, SMALL_W), lambda i: (0, 0)),
                  pl.BlockSpec((ts, ts), lambda i: (0, 0)),
                  pl.BlockSpec((ts, ts), lambda i: (0, 0))],
        out_specs=[pl.BlockSpec((ts, SMALL_W), lambda i: (i, 0)),
                   pl.BlockSpec((ts, SMALL_W), lambda i: (i, 0)),
                   pl.BlockSpec((SMALL_W, ts), lambda i: (0, i))],
        out_shape=[jax.ShapeDtypeStruct((t, SMALL_W), F32),
                   jax.ShapeDtypeStruct((t, SMALL_W), F32),
                   jax.ShapeDtypeStruct((SMALL_W, t), F32)],
        compiler_params=_cparams(("arbitrary",)),
        name="head_scalars",
    )(small, bias, alog, tril, eye)


def _ssd_kernel(z_ref, x_ref, b_ref, c_ref, p1_ref, p2_ref, arow_ref, e_ref, causal_ref, headbd_ref,
                xpar_ref, bpar_ref, cpar_ref, o_ref,
                xs_ref, bs_ref, cs_ref, st_ref):
    rows = x_ref.shape[0]
    gw = x_ref.shape[1]
    n_heads = gw // SSM_HEAD_DIM
    bias_row = slice(CONV_K, CONV_K + 1)
    dsk_row = slice(CONV_K + 1, CONV_K + 2)
    nw_row = slice(CONV_K + 2, CONV_K + 3)

    @pl.when(pl.program_id(2) == 0)
    def _():
        xs_ref[0:HALO, :] = jnp.zeros((HALO, gw), F32)
        bs_ref[0:HALO, :] = jnp.zeros((HALO, SSM_D_STATE), F32)
        cs_ref[0:HALO, :] = jnp.zeros((HALO, SSM_D_STATE), F32)
        st_ref[...] = jnp.zeros_like(st_ref)

    xa = _conv_silu(xs_ref, x_ref[...], xpar_ref, xpar_ref[bias_row, :], rows)
    ba = _conv_silu(bs_ref, b_ref[...], bpar_ref, bpar_ref[bias_row, :], rows)
    ca = _conv_silu(cs_ref, c_ref[...], cpar_ref, cpar_ref[bias_row, :], rows)

    e = e_ref[...]
    dt = _dot01_right(p1_ref[...], e)
    a_cum = _dot01_right(p2_ref[...], e)
    xdt = xa * dt
    ea = jnp.exp(a_cum)
    head_bd = headbd_ref[...]

    for ci in range(rows // CHUNK):
        sl = slice(ci * CHUNK, (ci + 1) * CHUNK)
        b_c, c_c = ba[sl], ca[sl]
        ac = a_cum[sl]
        a_last = ac[CHUNK - 1:CHUNK, :]
        xdt_c = xdt[sl]
        a_row = jnp.concatenate([arow_ref[h:h + 1, sl] for h in range(n_heads)], axis=1)
        decay = jnp.exp(jnp.minimum(ac - a_row, 0.0)) * causal_ref[...]
        cb = _bdot_nt(c_c, jnp.concatenate([b_c] * n_heads, axis=0))
        scores = cb * decay
        x_bd = jnp.concatenate([xdt_c.astype(BF16)] * n_heads, axis=0) * head_bd
        y_diag = jnp.dot(scores.astype(BF16), x_bd, preferred_element_type=F32)
        state = st_ref[...]
        y_off = _bdot(c_c, state) * ea[sl]
        xw = xdt_c * jnp.exp(a_last - ac)
        st_ref[...] = state * jnp.exp(a_last) + _bdot_tn(b_c, xw)
        y = y_diag + y_off + xpar_ref[dsk_row, :] * xa[sl]
        y = y * _silu_tanh(z_ref[sl, :])
        o_ref[sl, :] = (_rms(y) * xpar_ref[nw_row, :]).astype(o_ref.dtype)


def _ssd(proj, p1, p2, p2t, e_sel, conv_w, conv_b, d_skip, norm_w, bsz, seq, offs):
    d_in = norm_w.shape[0]
    gw = d_in // SSM_N_GROUPS
    hpg = gw // SSM_HEAD_DIM
    n = SSM_D_STATE
    ts = SSD_TILE
    nt = seq // ts
    zrow = jnp.zeros_like(conv_b)
    xtra = lambda v: jnp.concatenate([v, jnp.zeros((conv_b.shape[0] - d_in,), F32)])
    par = jnp.concatenate([conv_w, conv_b[None, :], xtra(jnp.repeat(d_skip, SSM_HEAD_DIM))[None, :],
                           xtra(norm_w)[None, :], zrow[None, :]], axis=0)
    assert par.shape[0] == SUBLANES
    z0 = offs["ssm_z"] // gw
    x0 = offs["ssm_x"] // gw
    b0 = offs["ssm_b"] // n
    c0 = offs["ssm_c"] // n
    cb0 = d_in // n
    cc0 = (d_in + SSM_N_GROUPS * n) // n
    row = lambda b, g, i: b * nt + i
    r = jnp.arange(CHUNK)[:, None]
    l = jnp.arange(gw)[None, :]
    causal = (r >= l % SSM_HEAD_DIM).astype(F32)
    k = jnp.arange(gw)
    head_bd = (k[:, None] // SSM_HEAD_DIM == k[None, :] // SSM_HEAD_DIM).astype(BF16)
    return pl.pallas_call(
        _ssd_kernel,
        grid=(bsz, SSM_N_GROUPS, nt),
        in_specs=[
            pl.BlockSpec((ts, gw), lambda b, g, i: (row(b, g, i), z0 + g)),
            pl.BlockSpec((ts, gw), lambda b, g, i: (row(b, g, i), x0 + g)),
            pl.BlockSpec((ts, n), lambda b, g, i: (row(b, g, i), b0 + g)),
            pl.BlockSpec((ts, n), lambda b, g, i: (row(b, g, i), c0 + g)),
            pl.BlockSpec((ts, SMALL_W), lambda b, g, i: (row(b, g, i), 0)),
            pl.BlockSpec((ts, SMALL_W), lambda b, g, i: (row(b, g, i), 0)),
            pl.BlockSpec((hpg, ts), lambda b, g, i: (g, row(b, g, i))),
            pl.BlockSpec((None, SPLIT_PARTS * SMALL_W, gw), lambda b, g, i: (g, 0, 0)),
            pl.BlockSpec((CHUNK, gw), lambda b, g, i: (0, 0)),
            pl.BlockSpec((gw, gw), lambda b, g, i: (0, 0)),
            pl.BlockSpec((SUBLANES, gw), lambda b, g, i: (0, g)),
            pl.BlockSpec((SUBLANES, n), lambda b, g, i: (0, cb0 + g)),
            pl.BlockSpec((SUBLANES, n), lambda b, g, i: (0, cc0 + g)),
        ],
        out_specs=pl.BlockSpec((ts, gw), lambda b, g, i: (row(b, g, i), g)),
        out_shape=jax.ShapeDtypeStruct((bsz * seq, d_in), BF16),
        scratch_shapes=[pltpu.VMEM((ts + HALO, gw), F32),
                        pltpu.VMEM((ts + HALO, n), F32),
                        pltpu.VMEM((ts + HALO, n), F32),
                        pltpu.VMEM((n, gw), F32)],
        compiler_params=_cparams(("arbitrary", "arbitrary", "arbitrary")),
        name="ssd_mixer",
    )(proj, proj, proj, proj, p1, p2, p2t, e_sel, causal, head_bd, par, par, par)


def _gdn_kernel(q_ref, k_ref, v_ref, z_ref, p1_ref, p2_ref, grow_ref, eb_ref, eg_ref,
                incl_ref, eye_ref, nbase_ref, e1_ref, e2_ref,
                cwq_ref, cwk_ref, cwv_ref, nw_ref, o_ref,
                qs_ref, ks_ref, vs_ref, st_ref):
    rows = q_ref.shape[0]
    dk = dv = GDN_HEAD_DIM
    n_qk = q_ref.shape[1] // dk
    n_v = v_ref.shape[1] // dv
    assert rows == 2 * dk

    @pl.when(pl.program_id(2) == 0)
    def _():
        qs_ref[0:HALO, :] = jnp.zeros((HALO, n_qk * dk), F32)
        ks_ref[0:HALO, :] = jnp.zeros((HALO, n_qk * dk), F32)
        vs_ref[0:HALO, :] = jnp.zeros((HALO, n_v * dv), F32)
        st_ref[...] = jnp.zeros_like(st_ref)

    q_all = _conv_silu(qs_ref, q_ref[...], cwq_ref, None, rows)
    k_all = _conv_silu(ks_ref, k_ref[...], cwk_ref, None, rows)
    v_all = _conv_silu(vs_ref, v_ref[...], cwv_ref, None, rows)
    beta = _dot01_right(p1_ref[...], eb_ref[...])
    gam = _dot01_right(p2_ref[...], eg_ref[...])
    e_gam = jnp.exp(gam)
    incl_bf = incl_ref[...]
    g_off = (pl.program_id(1) * n_v) % SUBLANES

    qs, ks, kks, qks = [], [], [], []
    for j in range(n_qk):
        js = slice(j * dk, (j + 1) * dk)
        q = q_all[:, js]
        k = k_all[:, js]
        q = q * (lax.rsqrt(jnp.sum(q * q, axis=-1, keepdims=True) + L2_EPS) * (dk ** -0.5))
        k = k * lax.rsqrt(jnp.sum(k * k, axis=-1, keepdims=True) + L2_EPS)
        qs.append(q)
        ks.append(k)
        kks.append(_bdot_nt(k, k))
        qks.append(_bdot_nt(q, k))

    heads = range(n_v)
    eye = eye_ref[...]
    t_inv, sq, e_lo, rhs, qkm, gbs = [], [], [], [], [], []
    for h in heads:
        j = h // GDN_V_PER_QK
        hs = slice(h * dv, (h + 1) * dv)
        gb = gam[:, hs]
        bb = beta[:, hs]
        seg = jnp.concatenate([gb, gb], axis=1) - grow_ref[pl.ds(g_off + h, 1), :]
        dec = jnp.exp(jnp.minimum(seg, 0.0))
        a = kks[j] * dec * jnp.concatenate([bb, bb], axis=1)
        b0 = a * nbase_ref[...]
        sq.append(b0)
        t_inv.append(eye + b0)
        a_bf = a.astype(BF16)
        e_lo.append([a_bf * m_ref[...] for m_ref in (e1_ref, e2_ref)])
        rhs.append(jnp.concatenate([v_all[:, hs] * bb, ks[j] * (bb * e_gam[:, hs])], axis=1))
        qkm.append((qks[j] * dec).astype(BF16) * incl_bf)
        gbs.append(gb)

    n_sq = INV_BASE.bit_length() - 2
    for h in heads:
        sq[h] = _bdot(sq[h], sq[h])
    for lvl in range(n_sq):
        for h in heads:
            if lvl == n_sq - 1:
                t_inv[h] = t_inv[h] + _bdot(t_inv[h], sq[h])
            else:
                res = _bdot(jnp.concatenate([t_inv[h], sq[h]], axis=0), sq[h])
                t_inv[h] = t_inv[h] + res[:rows]
                sq[h] = res[rows:]
    for lvl in range(len(e_lo[0])):
        for h in heads:
            t_inv[h] = t_inv[h] - _bdot(t_inv[h], _bdot(e_lo[h][lvl], t_inv[h]))
    xs = [_bdot(t_inv[h], rhs[h]) for h in heads]

    qd = [qs[h // GDN_V_PER_QK] * e_gam[:, h * dv:(h + 1) * dv] for h in heads]
    qs_parts = [[] for _ in heads]
    vn_parts = [[] for _ in heads]
    for ci in range(rows // CHUNK):
        sl = slice(ci * CHUNK, (ci + 1) * CHUNK)
        for h in heads:
            j = h // GDN_V_PER_QK
            u_c, w_c = xs[h][sl, :dv], xs[h][sl, dv:]
            gb = gbs[h]
            gl = gb[ci * CHUNK + CHUNK - 1:(ci + 1) * CHUNK, :]
            state = st_ref[h]
            wq = _bdot(jnp.concatenate([w_c, qd[h][sl]], axis=0), state)
            v_new = u_c - wq[:CHUNK]
            k_tail = ks[j][sl] * jnp.exp(gl - gb[sl])
            st_ref[h] = state * jnp.exp(gl) + _bdot_tn(k_tail, v_new)
            qs_parts[h].append(wq[CHUNK:])
            vn_parts[h].append(v_new)
    z = z_ref[...]
    for h in heads:
        hs = slice(h * dv, (h + 1) * dv)
        o = jnp.concatenate(qs_parts[h], axis=0) + jnp.dot(
            qkm[h], jnp.concatenate(vn_parts[h], axis=0).astype(BF16), preferred_element_type=F32)
        o = _rms(o) * nw_ref[...] * _silu_tanh(z[:, hs])
        o_ref[:, hs] = o.astype(o_ref.dtype)


def _gdn(proj, p1, p2, p2t, e_beta, e_gam, conv_w, norm_w, bsz, seq, offs, n_qk, g_row0):
    dk = GDN_HEAD_DIM
    qb = GDN_QK_PER_STEP
    vb = qb * GDN_V_PER_QK
    qw = qb * dk
    vw = vb * GDN_HEAD_DIM
    ts = SEQ_TILE
    nt = seq // ts
    q0 = offs["gdn_q"] // qw
    k0 = offs["gdn_k"] // qw
    v0 = offs["gdn_v"] // vw
    z0 = offs["gdn_z"] // vw
    ck0 = (n_qk * dk) // qw
    cv0 = (2 * n_qk * dk) // vw
    assert g_row0 % SUBLANES == 0 and (SUBLANES % vb == 0 or vb % SUBLANES == 0)
    row = lambda b, g, i: b * nt + i
    r = jnp.arange(ts)
    same = r[:, None] // CHUNK == r[None, :] // CHUNK
    lower = r[:, None] > r[None, :]
    incl = (same & (r[:, None] >= r[None, :])).astype(BF16)
    eye = (r[:, None] == r[None, :]).astype(F32)
    blk = lambda size: r[:, None] // size == r[None, :] // size
    nbase = -(blk(INV_BASE) & lower).astype(F32)
    e_masks = []
    size = INV_BASE
    while size < CHUNK:
        e_masks.append((blk(2 * size) & ~blk(size) & lower).astype(BF16))
        size *= 2
    assert len(e_masks) == 2
    return pl.pallas_call(
        _gdn_kernel,
        grid=(bsz, n_qk // qb, nt),
        in_specs=[
            pl.BlockSpec((ts, qw), lambda b, g, i: (row(b, g, i), q0 + g)),
            pl.BlockSpec((ts, qw), lambda b, g, i: (row(b, g, i), k0 + g)),
            pl.BlockSpec((ts, vw), lambda b, g, i: (row(b, g, i), v0 + g)),
            pl.BlockSpec((ts, vw), lambda b, g, i: (row(b, g, i), z0 + g)),
            pl.BlockSpec((ts, SMALL_W), lambda b, g, i: (row(b, g, i), 0)),
            pl.BlockSpec((ts, SMALL_W), lambda b, g, i: (row(b, g, i), 0)),
            pl.BlockSpec((max(vb, SUBLANES), ts),
                         lambda b, g, i: ((g_row0 + g * vb) // max(vb, SUBLANES), row(b, g, i))),
            pl.BlockSpec((None, SPLIT_PARTS * SMALL_W, vw), lambda b, g, i: (g, 0, 0)),
            pl.BlockSpec((None, SPLIT_PARTS * SMALL_W, vw), lambda b, g, i: (g, 0, 0)),
            pl.BlockSpec((ts, ts), lambda b, g, i: (0, 0)),
            pl.BlockSpec((ts, ts), lambda b, g, i: (0, 0)),
            pl.BlockSpec((ts, ts), lambda b, g, i: (0, 0)),
            pl.BlockSpec((ts, ts), lambda b, g, i: (0, 0)),
            pl.BlockSpec((ts, ts), lambda b, g, i: (0, 0)),
            pl.BlockSpec((CONV_K, qw), lambda b, g, i: (0, g)),
            pl.BlockSpec((CONV_K, qw), lambda b, g, i: (0, ck0 + g)),
            pl.BlockSpec((CONV_K, vw), lambda b, g, i: (0, cv0 + g)),
            pl.BlockSpec((1, dk), lambda b, g, i: (0, 0)),
        ],
        out_specs=pl.BlockSpec((ts, vw), lambda b, g, i: (row(b, g, i), g)),
        out_shape=jax.ShapeDtypeStruct((bsz * seq, n_qk * GDN_V_PER_QK * GDN_HEAD_DIM), BF16),
        scratch_shapes=[pltpu.VMEM((ts + HALO, qw), F32),
                        pltpu.VMEM((ts + HALO, qw), F32),
                        pltpu.VMEM((ts + HALO, vw), F32),
                        pltpu.VMEM((vb, dk, GDN_HEAD_DIM), F32)],
        compiler_params=_cparams(("arbitrary", "arbitrary", "arbitrary")),
        name="gdn_mixer",
    )(proj, proj, proj, proj, p1, p2, p2t, e_beta, e_gam, incl, eye, nbase, e_masks[0], e_masks[1],
      conv_w, conv_w, conv_w, norm_w)


def _merge_kernel(ys_ref, yg_ref, gs_ref, gg_ref, ws_ref, wg_ref, o_ref):
    a = jnp.dot(ys_ref[...], ws_ref[...], preferred_element_type=F32)
    b = jnp.dot(yg_ref[...], wg_ref[...], preferred_element_type=F32)
    o_ref[...] = (_sigmoid(gs_ref[...]) * a + _sigmoid(gg_ref[...]) * b).astype(o_ref.dtype)


def _merge(y_ssm, y_gdn, proj, w_ssm, w_gdn, offs):
    m, k = y_ssm.shape
    n = w_ssm.shape[1]
    tm, tn = 512, 512
    gs0 = offs["gate_ssm"] // tn
    gg0 = offs["gate_gdn"] // tn
    return pl.pallas_call(
        _merge_kernel,
        grid=(n // tn, m // tm),
        in_specs=[pl.BlockSpec((tm, k), lambda j, i: (i, 0)),
                  pl.BlockSpec((tm, k), lambda j, i: (i, 0)),
                  pl.BlockSpec((tm, tn), lambda j, i: (i, gs0 + j)),
                  pl.BlockSpec((tm, tn), lambda j, i: (i, gg0 + j)),
                  pl.BlockSpec((k, tn), lambda j, i: (0, j)),
                  pl.BlockSpec((k, tn), lambda j, i: (0, j))],
        out_specs=pl.BlockSpec((tm, tn), lambda j, i: (i, j)),
        out_shape=jax.ShapeDtypeStruct((m, n), BF16),
        compiler_params=_cparams(("arbitrary", "arbitrary")),
        name="merge_proj",
    )(y_ssm, y_gdn, proj, proj, w_ssm, w_gdn)


def _mixout_kernel(m_ref, wo_ref, x_ref, gate_ref, nw_ref, shift_ref, scale_ref, x1_ref, h2_ref):
    mix = jnp.dot(m_ref[...], wo_ref[...], preferred_element_type=F32)
    x1 = x_ref[...] + gate_ref[...] * mix
    x1_ref[...] = x1
    y = _rms(x1) * nw_ref[...]
    h2_ref[...] = (y * (1.0 + scale_ref[...]) + shift_ref[...]).astype(h2_ref.dtype)


def _mixout(merged, w_o, x, mod3, norm_w, gate_idx, shift_idx, scale_idx):
    bsz, seq, d = x.shape
    tm = 512
    nt = seq // tm
    return pl.pallas_call(
        _mixout_kernel,
        grid=(bsz, nt),
        in_specs=[pl.BlockSpec((tm, d), lambda b, i: (b * nt + i, 0)),
                  pl.BlockSpec((d, d), lambda b, i: (0, 0)),
                  pl.BlockSpec((None, tm, d), lambda b, i: (b, i, 0)),
                  pl.BlockSpec((None, 1, d), lambda b, i: (b, 0, gate_idx)),
                  pl.BlockSpec((1, d), lambda b, i: (0, 0)),
                  pl.BlockSpec((None, 1, d), lambda b, i: (b, 0, shift_idx)),
                  pl.BlockSpec((None, 1, d), lambda b, i: (b, 0, scale_idx))],
        out_specs=[pl.BlockSpec((None, tm, d), lambda b, i: (b, i, 0)),
                   pl.BlockSpec((None, tm, d), lambda b, i: (b, i, 0))],
        out_shape=[jax.ShapeDtypeStruct((bsz, seq, d), F32),
                   jax.ShapeDtypeStruct((bsz, seq, d), BF16)],
        compiler_params=_cparams(("arbitrary", "arbitrary")),
        name="mix_out",
    )(merged, w_o, x, mod3, norm_w.reshape(1, d), mod3, mod3)


def _gateup_kernel(h_ref, wg_ref, wu_ref, o_ref, wgb_ref, wub_ref):
    @pl.when(pl.program_id(1) == 0)
    def _():
        wgb_ref[...] = wg_ref[...].astype(BF16)
        wub_ref[...] = wu_ref[...].astype(BF16)

    h = h_ref[...]
    g = jnp.dot(h, wgb_ref[...], preferred_element_type=F32)
    u = jnp.dot(h, wub_ref[...], preferred_element_type=F32)
    o_ref[...] = (_silu(g) * u).astype(o_ref.dtype)


def _gateup(h, w_gate_up, layer):
    m, k = h.shape
    hid = w_gate_up.shape[2] // 2
    tm, tn = 1024, 512
    up0 = hid // tn
    return pl.pallas_call(
        _gateup_kernel,
        grid=(hid // tn, m // tm),
        in_specs=[pl.BlockSpec((tm, k), lambda j, i: (i, 0)),
                  pl.BlockSpec((None, k, tn), lambda j, i: (layer, 0, j)),
                  pl.BlockSpec((None, k, tn), lambda j, i: (layer, 0, up0 + j))],
        out_specs=pl.BlockSpec((tm, tn), lambda j, i: (i, j)),
        out_shape=jax.ShapeDtypeStruct((m, hid), BF16),
        scratch_shapes=[pltpu.VMEM((k, tn), BF16), pltpu.VMEM((k, tn), BF16)],
        compiler_params=_cparams(("arbitrary", "arbitrary")),
        name="ffn_gate_up",
    )(h, w_gate_up, w_gate_up)


def _down_kernel(a_ref, wd_ref, x1_ref, gate_ref, fw_ref, o_ref, *, final):
    ffn = jnp.dot(a_ref[...], wd_ref[...], preferred_element_type=F32)
    x2 = x1_ref[...] + gate_ref[...] * ffn
    if final:
        x2 = _rms(x2) * fw_ref[...]
    o_ref[...] = x2


def _down(act, w_down, x1, mod3, final_w, gate_idx, final):
    bsz, seq, d = x1.shape
    hid = act.shape[1]
    tm = 512
    nt = seq // tm
    return pl.pallas_call(
        functools.partial(_down_kernel, final=final),
        grid=(bsz, nt),
        in_specs=[pl.BlockSpec((tm, hid), lambda b, i: (b * nt + i, 0)),
                  pl.BlockSpec((hid, d), lambda b, i: (0, 0), pipeline_mode=pl.Buffered(1)),
                  pl.BlockSpec((None, tm, d), lambda b, i: (b, i, 0)),
                  pl.BlockSpec((None, 1, d), lambda b, i: (b, 0, gate_idx)),
                  pl.BlockSpec((1, d), lambda b, i: (0, 0))],
        out_specs=pl.BlockSpec((None, tm, d), lambda b, i: (b, i, 0)),
        out_shape=jax.ShapeDtypeStruct((bsz, seq, d), F32),
        compiler_params=_cparams(("arbitrary", "arbitrary")),
        name="ffn_down",
    )(act, w_down, x1, mod3, final_w.reshape(1, d))


def _selector(n_groups, src0, per_group, width, rep):
    k = jnp.arange(SPLIT_PARTS * SMALL_W)[None, :, None] % SMALL_W
    g = jnp.arange(n_groups)[:, None, None]
    c = jnp.arange(width)[None, None, :]
    return (k == src0 + g * per_group + c // rep).astype(BF16)


def kernel(x, c, w_ada, b_ada, norm_mix_w, w_in, ssm_conv_w, ssm_conv_b, ssm_dt_bias, ssm_a_log, ssm_d_skip, ssm_norm_w, gdn_conv_w, gdn_a_log, gdn_dt_bias, gdn_norm_w, w_ssm_proj, w_gdn_proj, w_o, norm_ffn_w, w_gate_up, w_down, final_norm_w):
    bsz, seq, d = x.shape
    depth = w_ada.shape[0]
    d_in = ssm_norm_w.shape[1]
    n_ssm_heads = ssm_dt_bias.shape[1]
    gn = SSM_N_GROUPS * SSM_D_STATE
    n_v = gdn_a_log.shape[1]
    n_qk = n_v // GDN_V_PER_QK
    qk_dim = n_qk * GDN_HEAD_DIM
    v_dim = n_v * GDN_HEAD_DIM
    assert n_ssm_heads + 2 * n_v == SMALL_W and seq % SEQ_TILE == 0 and seq % SSD_TILE == 0

    sizes = (d_in, d_in + 2 * gn, n_ssm_heads, 2 * qk_dim + v_dim, v_dim, n_v, n_v, d, d)
    starts = [0]
    for s in sizes:
        starts.append(starts[-1] + s)
    (s_z, s_xbc, s_dt, s_qkv, s_gz, s_beta, s_a, s_gs, s_gg) = starts[:-1]
    offs = {"ssm_z": 0, "ssm_x": d_in, "ssm_b": 2 * d_in, "ssm_c": 2 * d_in + gn}
    o_qkv = 2 * d_in + 2 * gn
    offs.update({"gdn_q": o_qkv, "gdn_k": o_qkv + qk_dim, "gdn_v": o_qkv + 2 * qk_dim})
    offs["gdn_z"] = o_qkv + 2 * qk_dim + v_dim
    offs["gate_ssm"] = offs["gdn_z"] + v_dim
    offs["gate_gdn"] = offs["gate_ssm"] + d

    vb = GDN_QK_PER_STEP * GDN_V_PER_QK
    e_dt = _selector(SSM_N_GROUPS, 0, n_ssm_heads // SSM_N_GROUPS, d_in // SSM_N_GROUPS, SSM_HEAD_DIM)
    e_beta = _selector(n_v // vb, n_ssm_heads, vb, vb * GDN_HEAD_DIM, GDN_HEAD_DIM)
    e_gam = _selector(n_v // vb, n_ssm_heads + n_v, vb, vb * GDN_HEAD_DIM, GDN_HEAD_DIM)

    out = x
    for layer in range(depth):
        wl = w_in[layer]
        w_small = jnp.concatenate([wl[:, s_dt:s_qkv], wl[:, s_beta:s_gs]], axis=1)

        mod = _adaln(c, w_ada[layer], b_ada[layer])
        mod3 = mod.reshape(bsz, 1, 6 * d)
        h = _norm_mod(out, norm_mix_w[layer], mod3, 0, 1).reshape(bsz * seq, d)
        proj = _inproj(h, w_in, layer, offs, offs["gate_gdn"] + d, s_qkv - s_dt, s_gs - s_gz - v_dim + s_qkv - s_dt)
        small = _matmul(h, w_small, 1024, SMALL_W, F32, "in_proj_small")

        zeros_b = jnp.zeros((n_v,), F32)
        bias = jnp.concatenate([ssm_dt_bias[layer], zeros_b, gdn_dt_bias[layer]])[None, :]
        alog = jnp.concatenate([ssm_a_log[layer], zeros_b, gdn_a_log[layer]])[None, :]
        p1, p2, p2t = _prep(small, bias, alog, n_ssm_heads, n_v)

        y_ssm = _ssd(proj, p1, p2, p2t, e_dt, ssm_conv_w[layer], ssm_conv_b[layer], ssm_d_skip[layer],
                     ssm_norm_w[layer], bsz, seq, offs)
        y_gdn = _gdn(proj, p1, p2, p2t, e_beta, e_gam, gdn_conv_w[layer], gdn_norm_w[layer][None, :],
                     bsz, seq, offs, n_qk, n_ssm_heads + n_v)

        merged = _merge(y_ssm, y_gdn, proj, w_ssm_proj[layer].astype(BF16), w_gdn_proj[layer].astype(BF16), offs)
        x1, h2 = _mixout(merged, w_o[layer].astype(BF16), out, mod3, norm_ffn_w[layer], 2, 3, 4)
        act = _gateup(h2.reshape(bsz * seq, d), w_gate_up, layer)
        out = _down(act, w_down[layer].astype(BF16), x1, mod3, final_norm_w, 5, layer == depth - 1)
    return out
```

```python
import functools

import jax
import jax.numpy as jnp
from jax import lax
from jax.experimental import pallas as pl
from jax.experimental.pallas import tpu as pltpu

F32 = jnp.float32
BF16 = jnp.bfloat16

NORM_EPS = 1e-6
L2_EPS = 1e-6
CONV_K = 4
SUBLANES = 8
HALO = 8
CHUNK = 64
INV_BASE = 16
SEQ_TILE = 256
SSD_TILE = 1024
SSM_HEAD_DIM = 64
SSM_N_GROUPS = 8
SSM_D_STATE = 128
GDN_HEAD_DIM = 128
GDN_V_PER_QK = 2
GDN_QK_PER_STEP = 8
SMALL_W = 128
SPLIT_PARTS = 3
INPROJ_ROWS = 256
VMEM_LIMIT = 56 * 1024 * 1024


def _cparams(sem):
    return pltpu.CompilerParams(dimension_semantics=sem, vmem_limit_bytes=VMEM_LIMIT)


def _bdot(a, b):
    return jnp.dot(a.astype(BF16), b.astype(BF16), preferred_element_type=F32)


def _bdot_nt(a, b):
    return lax.dot_general(a.astype(BF16), b.astype(BF16), (((1,), (1,)), ((), ())),
                           preferred_element_type=F32)


def _bdot_tn(a, b):
    return lax.dot_general(a.astype(BF16), b.astype(BF16), (((0,), (0,)), ((), ())),
                           preferred_element_type=F32)


def _split3(x):
    hi = x.astype(BF16)
    r1 = x - hi.astype(F32)
    mid = r1.astype(BF16)
    lo = (r1 - mid.astype(F32)).astype(BF16)
    return hi, mid, lo


def _dot01_left(m01, x):
    hi, mid, lo = _split3(x)
    d = lambda p: jnp.dot(m01, p, preferred_element_type=F32)
    return (d(lo) + d(mid)) + d(hi)


def _dot01_right(x, m01_x3):
    return jnp.dot(jnp.concatenate(_split3(x), axis=1), m01_x3, preferred_element_type=F32)


def _dot01_tn(x, m01):
    hi, mid, lo = _split3(x)
    d = lambda p: lax.dot_general(p, m01, (((0,), (0,)), ((), ())), preferred_element_type=F32)
    return (d(lo) + d(mid)) + d(hi)


def _sigmoid(x):
    return 1.0 / (1.0 + jnp.exp(-x))


def _silu(x):
    return x * _sigmoid(x)


def _silu_tanh(x):
    h = 0.5 * x
    return h * jnp.tanh(h) + h


def _softplus(x):
    return jnp.maximum(x, 0.0) + jnp.log(1.0 + jnp.exp(-jnp.abs(x)))


def _rms(x, eps=NORM_EPS):
    return x * lax.rsqrt(jnp.mean(x * x, axis=-1, keepdims=True) + eps)


def _conv_silu(stage_ref, cur, w_ref, bias, rows):
    stage_ref[HALO:HALO + rows, :] = cur
    acc = None
    for k in reversed(range(CONV_K)):
        start = HALO - (CONV_K - 1) + k
        term = stage_ref[start:start + rows, :] * w_ref[k:k + 1, :]
        acc = term if acc is None else acc + term
    if bias is not None:
        acc = acc + bias
    stage_ref[0:HALO, :] = cur[rows - HALO:rows, :]
    return _silu_tanh(acc)


def _ada_kernel(c_ref, w_ref, b_ref, o_ref):
    c_act = _silu(c_ref[...])
    o_ref[...] = _bdot(c_act, w_ref[...]) + b_ref[...]


def _adaln(c, w_ada, b_ada):
    bsz, d = c.shape
    n = w_ada.shape[1]
    tn = 1024
    return pl.pallas_call(
        _ada_kernel,
        grid=(n // tn,),
        in_specs=[pl.BlockSpec((bsz, d), lambda j: (0, 0)),
                  pl.BlockSpec((d, tn), lambda j: (0, j)),
                  pl.BlockSpec((1, tn), lambda j: (0, j))],
        out_specs=pl.BlockSpec((bsz, tn), lambda j: (0, j)),
        out_shape=jax.ShapeDtypeStruct((bsz, n), F32),
        compiler_params=_cparams(("arbitrary",)),
        name="adaln",
    )(c, w_ada, b_ada.reshape(1, n))


def _norm_mod_kernel(x_ref, w_ref, shift_ref, scale_ref, o_ref):
    y = _rms(x_ref[...]) * w_ref[...]
    o_ref[...] = (y * (1.0 + scale_ref[...]) + shift_ref[...]).astype(o_ref.dtype)


def _norm_mod(x, w, mod3, shift_idx, scale_idx):
    bsz, seq, d = x.shape
    tm = 512
    return pl.pallas_call(
        _norm_mod_kernel,
        grid=(bsz, seq // tm),
        in_specs=[pl.BlockSpec((None, tm, d), lambda b, i: (b, i, 0)),
                  pl.BlockSpec((1, d), lambda b, i: (0, 0)),
                  pl.BlockSpec((None, 1, d), lambda b, i: (b, 0, shift_idx)),
                  pl.BlockSpec((None, 1, d), lambda b, i: (b, 0, scale_idx))],
        out_specs=pl.BlockSpec((None, tm, d), lambda b, i: (b, i, 0)),
        out_shape=jax.ShapeDtypeStruct((bsz, seq, d), BF16),
        compiler_params=_cparams(("arbitrary", "arbitrary")),
        name="norm_mod",
    )(x, w.reshape(1, d), mod3, mod3)


def _mm_kernel(x_ref, w_ref, o_ref):
    w = w_ref[...].astype(x_ref.dtype)
    o_ref[...] = jnp.dot(x_ref[...], w, preferred_element_type=F32).astype(o_ref.dtype)


def _matmul(x, w, tm, tn, out_dtype, name):
    m, k = x.shape
    n = w.shape[1]
    return pl.pallas_call(
        _mm_kernel,
        grid=(n // tn, m // tm),
        in_specs=[pl.BlockSpec((tm, k), lambda j, i: (i, 0)),
                  pl.BlockSpec((k, tn), lambda j, i: (0, j))],
        out_specs=pl.BlockSpec((tm, tn), lambda j, i: (i, j)),
        out_shape=jax.ShapeDtypeStruct((m, n), out_dtype),
        compiler_params=_cparams(("arbitrary", "arbitrary")),
        name=name,
    )(x, w)


def _inproj_kernel(x_ref, wa0_ref, wa1_ref, wb_ref, o_ref, wbf_ref, *, n_shift1, n_shift2, shift1, shift2):
    n = pl.program_id(0)
    kdim = x_ref.shape[1]

    @pl.when(pl.program_id(1) == 0)
    def _():
        def repack(shift):
            half = kdim // 2
            for r0 in range(0, kdim, INPROJ_ROWS):
                rs = slice(r0, r0 + INPROJ_ROWS)
                wa_ref, ra = (wa0_ref, rs) if r0 < half else (wa1_ref, slice(r0 - half, r0 - half + INPROJ_ROWS))
                if shift == 0:
                    w = wa_ref[ra, :]
                else:
                    w = jnp.concatenate([wa_ref[ra, shift:], wb_ref[rs, :shift]], axis=1)
                wbf_ref[rs, :] = w.astype(BF16)

        @pl.when(n < n_shift1)
        def _():
            repack(0)

        @pl.when((n >= n_shift1) & (n < n_shift2))
        def _():
            repack(shift1)

        @pl.when(n >= n_shift2)
        def _():
            repack(shift2)

    o_ref[...] = jnp.dot(x_ref[...], wbf_ref[...], preferred_element_type=F32)


def _inproj(h, w_in, layer, offs, n_main, shift1, shift2):
    m, k = h.shape
    tm, tn = 1024, 1024
    segs = (offs["gate_ssm"], n_main, offs["gdn_q"])
    assert all(s % tn == 0 for s in segs) and shift2 == SMALL_W and 0 < shift1 < SMALL_W
    n_tiles = n_main // tn
    assert m // tm > 3
    tile = lambda j, i, after: jnp.minimum(j + (i >= after).astype(jnp.int32), n_tiles - 1)
    return pl.pallas_call(
        functools.partial(_inproj_kernel, n_shift1=offs["gdn_q"] // tn, n_shift2=offs["gate_ssm"] // tn,
                          shift1=shift1, shift2=shift2),
        grid=(n_main // tn, m // tm),
        in_specs=[pl.BlockSpec((tm, k), lambda j, i: (i, 0)),
                  pl.BlockSpec((None, k // 2, tn), lambda j, i: (layer, 0, tile(j, i, 1))),
                  pl.BlockSpec((None, k // 2, tn), lambda j, i: (layer, 1, tile(j, i, 2))),
                  pl.BlockSpec((None, k, SMALL_W), lambda j, i: (layer, 0, (tile(j, i, 3) + 1) * (tn // SMALL_W)))],
        out_specs=pl.BlockSpec((tm, tn), lambda j, i: (i, j)),
        out_shape=jax.ShapeDtypeStruct((m, n_main), F32),
        scratch_shapes=[pltpu.VMEM((k, tn), BF16)],
        compiler_params=_cparams(("arbitrary", "arbitrary")),
        name="in_proj",
    )(h, w_in, w_in, w_in)


def _prep_kernel(sm_ref, bias_ref, alog_ref, tril_ref, eye_ref, p1_ref, p2_ref, p2t_ref, *, n_dt, n_beta):
    sm = sm_ref[...]
    lane = lax.broadcasted_iota(jnp.int32, sm.shape, 1)
    is_dt = lane < n_dt
    is_beta = (lane >= n_dt) & (lane < n_dt + n_beta)
    sp = _softplus(sm + bias_ref[...])
    rate = -jnp.exp(alog_ref[...]) * sp
    p1_ref[...] = jnp.where(is_beta, _sigmoid(sm), jnp.where(is_dt, sp, rate))
    cum = _dot01_left(tril_ref[...], jnp.where(is_beta, 0.0, rate))
    p2_ref[...] = cum
    p2t_ref[...] = _dot01_tn(cum, eye_ref[...])


def _prep(small, bias, alog, n_dt, n_beta):
    t = small.shape[0]
    ts = SEQ_TILE
    r = jnp.arange(ts)
    tril = ((r[:, None] // CHUNK == r[None, :] // CHUNK) & (r[:, None] >= r[None, :])).astype(BF16)
    eye = (r[:, None] == r[None, :]).astype(BF16)
    return pl.pallas_call(
        functools.partial(_prep_kernel, n_dt=n_dt, n_beta=n_beta),
        grid=(t // ts,),
        in_specs=[pl.BlockSpec((ts, SMALL_W), lambda i: (i, 0)),
                  pl.BlockSpec((1, SMALL_W), lambda i: (0, 0)),
                  pl.BlockSpec((1, SMALL_W), lambda i: (0, 0)),
                  pl.BlockSpec((ts, ts), lambda i: (0, 0)),
                  pl.BlockSpec((ts, ts), lambda i: (0, 0))],
        out_specs=[pl.BlockSpec((ts, SMALL_W), lambda i: (i, 0)),
                   pl.BlockSpec((ts, SMALL_W), lambda i: (i, 0)),
                   pl.BlockSpec((SMALL_W, ts), lambda i: (0, i))],
        out_shape=[jax.ShapeDtypeStruct((t, SMALL_W), F32),
                   jax.ShapeDtypeStruct((t, SMALL_W), F32),
                   jax.ShapeDtypeStruct((SMALL_W, t), F32)],
        compiler_params=_cparams(("arbitrary",)),
        name="head_scalars",
    )(small, bias, alog, tril, eye)


def _ssd_kernel(z_ref, x_ref, b_ref, c_ref, p1_ref, p2_ref, arow_ref, e_ref, causal_ref, headbd_ref,
                xpar_ref, bpar_ref, cpar_ref, o_ref,
                xs_ref, bs_ref, cs_ref, st_ref):
    rows = x_ref.shape[0]
    gw = x_ref.shape[1]
    n_heads = gw // SSM_HEAD_DIM
    bias_row = slice(CONV_K, CONV_K + 1)
    dsk_row = slice(CONV_K + 1, CONV_K + 2)
    nw_row = slice(CONV_K + 2, CONV_K + 3)

    @pl.when(pl.program_id(2) == 0)
    def _():
        xs_ref[0:HALO, :] = jnp.zeros((HALO, gw), F32)
        bs_ref[0:HALO, :] = jnp.zeros((HALO, SSM_D_STATE), F32)
        cs_ref[0:HALO, :] = jnp.zeros((HALO, SSM_D_STATE), F32)
        st_ref[...] = jnp.zeros_like(st_ref)

    xa = _conv_silu(xs_ref, x_ref[...], xpar_ref, xpar_ref[bias_row, :], rows)
    ba = _conv_silu(bs_ref, b_ref[...], bpar_ref, bpar_ref[bias_row, :], rows)
    ca = _conv_silu(cs_ref, c_ref[...], cpar_ref, cpar_ref[bias_row, :], rows)

    e = e_ref[...]
    dt = _dot01_right(p1_ref[...], e)
    a_cum = _dot01_right(p2_ref[...], e)
    xdt = xa * dt
    ea = jnp.exp(a_cum)
    head_bd = headbd_ref[...]

    for ci in range(rows // CHUNK):
        sl = slice(ci * CHUNK, (ci + 1) * CHUNK)
        b_c, c_c = ba[sl], ca[sl]
        ac = a_cum[sl]
        a_last = ac[CHUNK - 1:CHUNK, :]
        xdt_c = xdt[sl]
        a_row = jnp.concatenate([arow_ref[h:h + 1, sl] for h in range(n_heads)], axis=1)
        decay = jnp.exp(jnp.minimum(ac - a_row, 0.0)) * causal_ref[...]
        cb = _bdot_nt(c_c, jnp.concatenate([b_c] * n_heads, axis=0))
        scores = cb * decay
        x_bd = jnp.concatenate([xdt_c.astype(BF16)] * n_heads, axis=0) * head_bd
        y_diag = jnp.dot(scores.astype(BF16), x_bd, preferred_element_type=F32)
        state = st_ref[...]
        y_off = _bdot(c_c, state) * ea[sl]
        xw = xdt_c * jnp.exp(a_last - ac)
        st_ref[...] = state * jnp.exp(a_last) + _bdot_tn(b_c, xw)
        y = y_diag + y_off + xpar_ref[dsk_row, :] * xa[sl]
        y = y * _silu_tanh(z_ref[sl, :])
        o_ref[sl, :] = (_rms(y) * xpar_ref[nw_row, :]).astype(o_ref.dtype)


def _ssd(proj, p1, p2, p2t, e_sel, conv_w, conv_b, d_skip, norm_w, bsz, seq, offs):
    d_in = norm_w.shape[0]
    gw = d_in // SSM_N_GROUPS
    hpg = gw // SSM_HEAD_DIM
    n = SSM_D_STATE
    ts = SSD_TILE
    nt = seq // ts
    zrow = jnp.zeros_like(conv_b)
    xtra = lambda v: jnp.concatenate([v, jnp.zeros((conv_b.shape[0] - d_in,), F32)])
    par = jnp.concatenate([conv_w, conv_b[None, :], xtra(jnp.repeat(d_skip, SSM_HEAD_DIM))[None, :],
                           xtra(norm_w)[None, :], zrow[None, :]], axis=0)
    assert par.shape[0] == SUBLANES
    z0 = offs["ssm_z"] // gw
    x0 = offs["ssm_x"] // gw
    b0 = offs["ssm_b"] // n
    c0 = offs["ssm_c"] // n
    cb0 = d_in // n
    cc0 = (d_in + SSM_N_GROUPS * n) // n
    row = lambda b, g, i: b * nt + i
    r = jnp.arange(CHUNK)[:, None]
    l = jnp.arange(gw)[None, :]
    causal = (r >= l % SSM_HEAD_DIM).astype(F32)
    k = jnp.arange(gw)
    head_bd = (k[:, None] // SSM_HEAD_DIM == k[None, :] // SSM_HEAD_DIM).astype(BF16)
    return pl.pallas_call(
        _ssd_kernel,
        grid=(bsz, SSM_N_GROUPS, nt),
        in_specs=[
            pl.BlockSpec((ts, gw), lambda b, g, i: (row(b, g, i), z0 + g)),
            pl.BlockSpec((ts, gw), lambda b, g, i: (row(b, g, i), x0 + g)),
            pl.BlockSpec((ts, n), lambda b, g, i: (row(b, g, i), b0 + g)),
            pl.BlockSpec((ts, n), lambda b, g, i: (row(b, g, i), c0 + g)),
            pl.BlockSpec((ts, SMALL_W), lambda b, g, i: (row(b, g, i), 0)),
            pl.BlockSpec((ts, SMALL_W), lambda b, g, i: (row(b, g, i), 0)),
            pl.BlockSpec((hpg, ts), lambda b, g, i: (g, row(b, g, i))),
            pl.BlockSpec((None, SPLIT_PARTS * SMALL_W, gw), lambda b, g, i: (g, 0, 0)),
            pl.BlockSpec((CHUNK, gw), lambda b, g, i: (0, 0)),
            pl.BlockSpec((gw, gw), lambda b, g, i: (0, 0)),
            pl.BlockSpec((SUBLANES, gw), lambda b, g, i: (0, g)),
            pl.BlockSpec((SUBLANES, n), lambda b, g, i: (0, cb0 + g)),
            pl.BlockSpec((SUBLANES, n), lambda b, g, i: (0, cc0 + g)),
        ],
        out_specs=pl.BlockSpec((ts, gw), lambda b, g, i: (row(b, g, i), g)),
        out_shape=jax.ShapeDtypeStruct((bsz * seq, d_in), BF16),
        scratch_shapes=[pltpu.VMEM((ts + HALO, gw), F32),
                        pltpu.VMEM((ts + HALO, n), F32),
                        pltpu.VMEM((ts + HALO, n), F32),
                        pltpu.VMEM((n, gw), F32)],
        compiler_params=_cparams(("arbitrary", "arbitrary", "arbitrary")),
        name="ssd_mixer",
    )(proj, proj, proj, proj, p1, p2, p2t, e_sel, causal, head_bd, par, par, par)


def _gdn_kernel(q_ref, k_ref, v_ref, z_ref, p1_ref, p2_ref, grow_ref, eb_ref, eg_ref,
                incl_ref, eye_ref, nbase_ref, e1_ref, e2_ref,
                cwq_ref, cwk_ref, cwv_ref, nw_ref, o_ref,
                qs_ref, ks_ref, vs_ref, st_ref):
    rows = q_ref.shape[0]
    dk = dv = GDN_HEAD_DIM
    n_qk = q_ref.shape[1] // dk
    n_v = v_ref.shape[1] // dv
    assert rows == 2 * dk

    @pl.when(pl.program_id(2) == 0)
    def _():
        qs_ref[0:HALO, :] = jnp.zeros((HALO, n_qk * dk), F32)
        ks_ref[0:HALO, :] = jnp.zeros((HALO, n_qk * dk), F32)
        vs_ref[0:HALO, :] = jnp.zeros((HALO, n_v * dv), F32)
        st_ref[...] = jnp.zeros_like(st_ref)

    q_all = _conv_silu(qs_ref, q_ref[...], cwq_ref, None, rows)
    k_all = _conv_silu(ks_ref, k_ref[...], cwk_ref, None, rows)
    v_all = _conv_silu(vs_ref, v_ref[...], cwv_ref, None, rows)
    beta = _dot01_right(p1_ref[...], eb_ref[...])
    gam = _dot01_right(p2_ref[...], eg_ref[...])
    e_gam = jnp.exp(gam)
    incl_bf = incl_ref[...]
    g_off = (pl.program_id(1) * n_v) % SUBLANES

    qs, ks, kks, qks = [], [], [], []
    for j in range(n_qk):
        js = slice(j * dk, (j + 1) * dk)
        q = q_all[:, js]
        k = k_all[:, js]
        q = q * (lax.rsqrt(jnp.sum(q * q, axis=-1, keepdims=True) + L2_EPS) * (dk ** -0.5))
        k = k * lax.rsqrt(jnp.sum(k * k, axis=-1, keepdims=True) + L2_EPS)
        qs.append(q)
        ks.append(k)
        kks.append(_bdot_nt(k, k))
        qks.append(_bdot_nt(q, k))

    heads = range(n_v)
    eye = eye_ref[...]
    t_inv, sq, e_lo, rhs, qkm, gbs = [], [], [], [], [], []
    for h in heads:
        j = h // GDN_V_PER_QK
        hs = slice(h * dv, (h + 1) * dv)
        gb = gam[:, hs]
        bb = beta[:, hs]
        seg = jnp.concatenate([gb, gb], axis=1) - grow_ref[pl.ds(g_off + h, 1), :]
        dec = jnp.exp(jnp.minimum(seg, 0.0))
        a = kks[j] * dec * jnp.concatenate([bb, bb], axis=1)
        b0 = a * nbase_ref[...]
        sq.append(b0)
        t_inv.append(eye + b0)
        a_bf = a.astype(BF16)
        e_lo.append([a_bf * m_ref[...] for m_ref in (e1_ref, e2_ref)])
        rhs.append(jnp.concatenate([v_all[:, hs] * bb, ks[j] * (bb * e_gam[:, hs])], axis=1))
        qkm.append((qks[j] * dec).astype(BF16) * incl_bf)
        gbs.append(gb)

    n_sq = INV_BASE.bit_length() - 2
    for h in heads:
        sq[h] = _bdot(sq[h], sq[h])
    for lvl in range(n_sq):
        for h in heads:
            if lvl == n_sq - 1:
                t_inv[h] = t_inv[h] + _bdot(t_inv[h], sq[h])
            else:
                res = _bdot(jnp.concatenate([t_inv[h], sq[h]], axis=0), sq[h])
                t_inv[h] = t_inv[h] + res[:rows]
                sq[h] = res[rows:]
    for lvl in range(len(e_lo[0])):
        for h in heads:
            t_inv[h] = t_inv[h] - _bdot(t_inv[h], _bdot(e_lo[h][lvl], t_inv[h]))
    xs = [_bdot(t_inv[h], rhs[h]) for h in heads]

    qd = [qs[h // GDN_V_PER_QK] * e_gam[:, h * dv:(h + 1) * dv] for h in heads]
    qs_parts = [[] for _ in heads]
    vn_parts = [[] for _ in heads]
    for ci in range(rows // CHUNK):
        sl = slice(ci * CHUNK, (ci + 1) * CHUNK)
        for h in heads:
            j = h // GDN_V_PER_QK
            u_c, w_c = xs[h][sl, :dv], xs[h][sl, dv:]
            gb = gbs[h]
            gl = gb[ci * CHUNK + CHUNK - 1:(ci + 1) * CHUNK, :]
            state = st_ref[h]
            wq = _bdot(jnp.concatenate([w_c, qd[h][sl]], axis=0), state)
            v_new = u_c - wq[:CHUNK]
            k_tail = ks[j][sl] * jnp.exp(gl - gb[sl])
            st_ref[h] = state * jnp.exp(gl) + _bdot_tn(k_tail, v_new)
            qs_parts[h].append(wq[CHUNK:])
            vn_parts[h].append(v_new)
    z = z_ref[...]
    for h in heads:
        hs = slice(h * dv, (h + 1) * dv)
        o = jnp.concatenate(qs_parts[h], axis=0) + jnp.dot(
            qkm[h], jnp.concatenate(vn_parts[h], axis=0).astype(BF16), preferred_element_type=F32)
        o = _rms(o) * nw_ref[...] * _silu_tanh(z[:, hs])
        o_ref[:, hs] = o.astype(o_ref.dtype)


def _gdn(proj, p1, p2, p2t, e_beta, e_gam, conv_w, norm_w, bsz, seq, offs, n_qk, g_row0):
    dk = GDN_HEAD_DIM
    qb = GDN_QK_PER_STEP
    vb = qb * GDN_V_PER_QK
    qw = qb * dk
    vw = vb * GDN_HEAD_DIM
    ts = SEQ_TILE
    nt = seq // ts
    q0 = offs["gdn_q"] // qw
    k0 = offs["gdn_k"] // qw
    v0 = offs["gdn_v"] // vw
    z0 = offs["gdn_z"] // vw
    ck0 = (n_qk * dk) // qw
    cv0 = (2 * n_qk * dk) // vw
    assert g_row0 % SUBLANES == 0 and (SUBLANES % vb == 0 or vb % SUBLANES == 0)
    row = lambda b, g, i: b * nt + i
    r = jnp.arange(ts)
    same = r[:, None] // CHUNK == r[None, :] // CHUNK
    lower = r[:, None] > r[None, :]
    incl = (same & (r[:, None] >= r[None, :])).astype(BF16)
    eye = (r[:, None] == r[None, :]).astype(F32)
    blk = lambda size: r[:, None] // size == r[None, :] // size
    nbase = -(blk(INV_BASE) & lower).astype(F32)
    e_masks = []
    size = INV_BASE
    while size < CHUNK:
        e_masks.append((blk(2 * size) & ~blk(size) & lower).astype(BF16))
        size *= 2
    assert len(e_masks) == 2
    return pl.pallas_call(
        _gdn_kernel,
        grid=(bsz, n_qk // qb, nt),
        in_specs=[
            pl.BlockSpec((ts, qw), lambda b, g, i: (row(b, g, i), q0 + g)),
            pl.BlockSpec((ts, qw), lambda b, g, i: (row(b, g, i), k0 + g)),
            pl.BlockSpec((ts, vw), lambda b, g, i: (row(b, g, i), v0 + g)),
            pl.BlockSpec((ts, vw), lambda b, g, i: (row(b, g, i), z0 + g)),
            pl.BlockSpec((ts, SMALL_W), lambda b, g, i: (row(b, g, i), 0)),
            pl.BlockSpec((ts, SMALL_W), lambda b, g, i: (row(b, g, i), 0)),
            pl.BlockSpec((max(vb, SUBLANES), ts),
                         lambda b, g, i: ((g_row0 + g * vb) // max(vb, SUBLANES), row(b, g, i))),
            pl.BlockSpec((None, SPLIT_PARTS * SMALL_W, vw), lambda b, g, i: (g, 0, 0)),
            pl.BlockSpec((None, SPLIT_PARTS * SMALL_W, vw), lambda b, g, i: (g, 0, 0)),
            pl.BlockSpec((ts, ts), lambda b, g, i: (0, 0)),
            pl.BlockSpec((ts, ts), lambda b, g, i: (0, 0)),
            pl.BlockSpec((ts, ts), lambda b, g, i: (0, 0)),
            pl.BlockSpec((ts, ts), lambda b, g, i: (0, 0)),
            pl.BlockSpec((ts, ts), lambda b, g, i: (0, 0)),
            pl.BlockSpec((CONV_K, qw), lambda b, g, i: (0, g)),
            pl.BlockSpec((CONV_K, qw), lambda b, g, i: (0, ck0 + g)),
            pl.BlockSpec((CONV_K, vw), lambda b, g, i: (0, cv0 + g)),
            pl.BlockSpec((1, dk), lambda b, g, i: (0, 0)),
        ],
        out_specs=pl.BlockSpec((ts, vw), lambda b, g, i: (row(b, g, i), g)),
        out_shape=jax.ShapeDtypeStruct((bsz * seq, n_qk * GDN_V_PER_QK * GDN_HEAD_DIM), BF16),
        scratch_shapes=[pltpu.VMEM((ts + HALO, qw), F32),
                        pltpu.VMEM((ts + HALO, qw), F32),
                        pltpu.VMEM((ts + HALO, vw), F32),
                        pltpu.VMEM((vb, dk, GDN_HEAD_DIM), F32)],
        compiler_params=_cparams(("arbitrary", "arbitrary", "arbitrary")),
        name="gdn_mixer",
    )(proj, proj, proj, proj, p1, p2, p2t, e_beta, e_gam, incl, eye, nbase, e_masks[0], e_masks[1],
      conv_w, conv_w, conv_w, norm_w)


def _merge_kernel(ys_ref, yg_ref, gs_ref, gg_ref, ws_ref, wg_ref, o_ref):
    a = jnp.dot(ys_ref[...], ws_ref[...], preferred_element_type=F32)
    b = jnp.dot(yg_ref[...], wg_ref[...], preferred_element_type=F32)
    o_ref[...] = (_sigmoid(gs_ref[...]) * a + _sigmoid(gg_ref[...]) * b).astype(o_ref.dtype)


def _merge(y_ssm, y_gdn, proj, w_ssm, w_gdn, offs):
    m, k = y_ssm.shape
    n = w_ssm.shape[1]
    tm, tn = 512, 512
    gs0 = offs["gate_ssm"] // tn
    gg0 = offs["gate_gdn"] // tn
    return pl.pallas_call(
        _merge_kernel,
        grid=(n // tn, m // tm),
        in_specs=[pl.BlockSpec((tm, k), lambda j, i: (i, 0)),
                  pl.BlockSpec((tm, k), lambda j, i: (i, 0)),
                  pl.BlockSpec((tm, tn), lambda j, i: (i, gs0 + j)),
                  pl.BlockSpec((tm, tn), lambda j, i: (i, gg0 + j)),
                  pl.BlockSpec((k, tn), lambda j, i: (0, j)),
                  pl.BlockSpec((k, tn), lambda j, i: (0, j))],
        out_specs=pl.BlockSpec((tm, tn), lambda j, i: (i, j)),
        out_shape=jax.ShapeDtypeStruct((m, n), BF16),
        compiler_params=_cparams(("arbitrary", "arbitrary")),
        name="merge_proj",
    )(y_ssm, y_gdn, proj, proj, w_ssm, w_gdn)


def _mixout_kernel(m_ref, wo_ref, x_ref, gate_ref, nw_ref, shift_ref, scale_ref, x1_ref, h2_ref):
    mix = jnp.dot(m_ref[...], wo_ref[...], preferred_element_type=F32)
    x1 = x_ref[...] + gate_ref[...] * mix
    x1_ref[...] = x1
    y = _rms(x1) * nw_ref[...]
    h2_ref[...] = (y * (1.0 + scale_ref[...]) + shift_ref[...]).astype(h2_ref.dtype)


def _mixout(merged, w_o, x, mod3, norm_w, gate_idx, shift_idx, scale_idx):
    bsz, seq, d = x.shape
    tm = 512
    nt = seq // tm
    return pl.pallas_call(
        _mixout_kernel,
        grid=(bsz, nt),
        in_specs=[pl.BlockSpec((tm, d), lambda b, i: (b * nt + i, 0)),
                  pl.BlockSpec((d, d), lambda b, i: (0, 0)),
                  pl.BlockSpec((None, tm, d), lambda b, i: (b, i, 0)),
                  pl.BlockSpec((None, 1, d), lambda b, i: (b, 0, gate_idx)),
                  pl.BlockSpec((1, d), lambda b, i: (0, 0)),
                  pl.BlockSpec((None, 1, d), lambda b, i: (b, 0, shift_idx)),
                  pl.BlockSpec((None, 1, d), lambda b, i: (b, 0, scale_idx))],
        out_specs=[pl.BlockSpec((None, tm, d), lambda b, i: (b, i, 0)),
                   pl.BlockSpec((None, tm, d), lambda b, i: (b, i, 0))],
        out_shape=[jax.ShapeDtypeStruct((bsz, seq, d), F32),
                   jax.ShapeDtypeStruct((bsz, seq, d), BF16)],
        compiler_params=_cparams(("arbitrary", "arbitrary")),
        name="mix_out",
    )(merged, w_o, x, mod3, norm_w.reshape(1, d), mod3, mod3)


def _gateup_kernel(h_ref, wg_ref, wu_ref, o_ref, wgb_ref, wub_ref):
    @pl.when(pl.program_id(1) == 0)
    def _():
        wgb_ref[...] = wg_ref[...].astype(BF16)
        wub_ref[...] = wu_ref[...].astype(BF16)

    h = h_ref[...]
    g = jnp.dot(h, wgb_ref[...], preferred_element_type=F32)
    u = jnp.dot(h, wub_ref[...], preferred_element_type=F32)
    o_ref[...] = (_silu(g) * u).astype(o_ref.dtype)


def _gateup(h, w_gate_up, layer):
    m, k = h.shape
    hid = w_gate_up.shape[2] // 2
    tm, tn = 1024, 512
    up0 = hid // tn
    assert m // tm > 2
    tile = lambda j, i, after: jnp.minimum(j + (i >= after).astype(jnp.int32), up0 - 1)
    return pl.pallas_call(
        _gateup_kernel,
        grid=(hid // tn, m // tm),
        in_specs=[pl.BlockSpec((tm, k), lambda j, i: (i, 0)),
                  pl.BlockSpec((None, k, tn), lambda j, i: (layer, 0, tile(j, i, 1))),
                  pl.BlockSpec((None, k, tn), lambda j, i: (layer, 0, up0 + tile(j, i, 2)))],
        out_specs=pl.BlockSpec((tm, tn), lambda j, i: (i, j)),
        out_shape=jax.ShapeDtypeStruct((m, hid), BF16),
        scratch_shapes=[pltpu.VMEM((k, tn), BF16), pltpu.VMEM((k, tn), BF16)],
        compiler_params=_cparams(("arbitrary", "arbitrary")),
        name="ffn_gate_up",
    )(h, w_gate_up, w_gate_up)


def _down_kernel(a_ref, wd_ref, x1_ref, gate_ref, fw_ref, o_ref, *, final):
    ffn = jnp.dot(a_ref[...], wd_ref[...], preferred_element_type=F32)
    x2 = x1_ref[...] + gate_ref[...] * ffn
    if final:
        x2 = _rms(x2) * fw_ref[...]
    o_ref[...] = x2


def _down(act, w_down, x1, mod3, final_w, gate_idx, final):
    bsz, seq, d = x1.shape
    hid = act.shape[1]
    tm = 512
    nt = seq // tm
    return pl.pallas_call(
        functools.partial(_down_kernel, final=final),
        grid=(bsz, nt),
        in_specs=[pl.BlockSpec((tm, hid), lambda b, i: (b * nt + i, 0)),
                  pl.BlockSpec((hid, d), lambda b, i: (0, 0), pipeline_mode=pl.Buffered(1)),
                  pl.BlockSpec((None, tm, d), lambda b, i: (b, i, 0)),
                  pl.BlockSpec((None, 1, d), lambda b, i: (b, 0, gate_idx)),
                  pl.BlockSpec((1, d), lambda b, i: (0, 0))],
        out_specs=pl.BlockSpec((None, tm, d), lambda b, i: (b, i, 0)),
        out_shape=jax.ShapeDtypeStruct((bsz, seq, d), F32),
        compiler_params=_cparams(("arbitrary", "arbitrary")),
        name="ffn_down",
    )(act, w_down, x1, mod3, final_w.reshape(1, d))


def _selector(n_groups, src0, per_group, width, rep):
    k = jnp.arange(SPLIT_PARTS * SMALL_W)[None, :, None] % SMALL_W
    g = jnp.arange(n_groups)[:, None, None]
    c = jnp.arange(width)[None, None, :]
    return (k == src0 + g * per_group + c // rep).astype(BF16)


def kernel(x, c, w_ada, b_ada, norm_mix_w, w_in, ssm_conv_w, ssm_conv_b, ssm_dt_bias, ssm_a_log, ssm_d_skip, ssm_norm_w, gdn_conv_w, gdn_a_log, gdn_dt_bias, gdn_norm_w, w_ssm_proj, w_gdn_proj, w_o, norm_ffn_w, w_gate_up, w_down, final_norm_w):
    bsz, seq, d = x.shape
    depth = w_ada.shape[0]
    d_in = ssm_norm_w.shape[1]
    n_ssm_heads = ssm_dt_bias.shape[1]
    gn = SSM_N_GROUPS * SSM_D_STATE
    n_v = gdn_a_log.shape[1]
    n_qk = n_v // GDN_V_PER_QK
    qk_dim = n_qk * GDN_HEAD_DIM
    v_dim = n_v * GDN_HEAD_DIM
    assert n_ssm_heads + 2 * n_v == SMALL_W and seq % SEQ_TILE == 0 and seq % SSD_TILE == 0

    sizes = (d_in, d_in + 2 * gn, n_ssm_heads, 2 * qk_dim + v_dim, v_dim, n_v, n_v, d, d)
    starts = [0]
    for s in sizes:
        starts.append(starts[-1] + s)
    (s_z, s_xbc, s_dt, s_qkv, s_gz, s_beta, s_a, s_gs, s_gg) = starts[:-1]
    offs = {"ssm_z": 0, "ssm_x": d_in, "ssm_b": 2 * d_in, "ssm_c": 2 * d_in + gn}
    o_qkv = 2 * d_in + 2 * gn
    offs.update({"gdn_q": o_qkv, "gdn_k": o_qkv + qk_dim, "gdn_v": o_qkv + 2 * qk_dim})
    offs["gdn_z"] = o_qkv + 2 * qk_dim + v_dim
    offs["gate_ssm"] = offs["gdn_z"] + v_dim
    offs["gate_gdn"] = offs["gate_ssm"] + d

    vb = GDN_QK_PER_STEP * GDN_V_PER_QK
    e_dt = _selector(SSM_N_GROUPS, 0, n_ssm_heads // SSM_N_GROUPS, d_in // SSM_N_GROUPS, SSM_HEAD_DIM)
    e_beta = _selector(n_v // vb, n_ssm_heads, vb, vb * GDN_HEAD_DIM, GDN_HEAD_DIM)
    e_gam = _selector(n_v // vb, n_ssm_heads + n_v, vb, vb * GDN_HEAD_DIM, GDN_HEAD_DIM)

    out = x
    for layer in range(depth):
        wl = w_in[layer]
        w_small = jnp.concatenate([wl[:, s_dt:s_qkv], wl[:, s_beta:s_gs]], axis=1)

        mod = _adaln(c, w_ada[layer], b_ada[layer])
        mod3 = mod.reshape(bsz, 1, 6 * d)
        h = _norm_mod(out, norm_mix_w[layer], mod3, 0, 1).reshape(bsz * seq, d)
        proj = _inproj(h, w_in, layer, offs, offs["gate_gdn"] + d, s_qkv - s_dt, s_gs - s_gz - v_dim + s_qkv - s_dt)
        small = _matmul(h, w_small, 1024, SMALL_W, F32, "in_proj_small")

        zeros_b = jnp.zeros((n_v,), F32)
        bias = jnp.concatenate([ssm_dt_bias[layer], zeros_b, gdn_dt_bias[layer]])[None, :]
        alog = jnp.concatenate([ssm_a_log[layer], zeros_b, gdn_a_log[layer]])[None, :]
        p1, p2, p2t = _prep(small, bias, alog, n_ssm_heads, n_v)

        y_ssm = _ssd(proj, p1, p2, p2t, e_dt, ssm_conv_w[layer], ssm_conv_b[layer], ssm_d_skip[layer],
                     ssm_norm_w[layer], bsz, seq, offs)
        y_gdn = _gdn(proj, p1, p2, p2t, e_beta, e_gam, gdn_conv_w[layer], gdn_norm_w[layer][None, :],
                     bsz, seq, offs, n_qk, n_ssm_heads + n_v)

        merged = _merge(y_ssm, y_gdn, proj, w_ssm_proj[layer].astype(BF16), w_gdn_proj[layer].astype(BF16), offs)
        x1, h2 = _mixout(merged, w_o[layer].astype(BF16), out, mod3, norm_ffn_w[layer], 2, 3, 4)
        act = _gateup(h2.reshape(bsz * seq, d), w_gate_up, layer)
        out = _down(act, w_down[layer].astype(BF16), x1, mod3, final_norm_w, 5, layer == depth - 1)
    return out
```

```python
import functools

import jax
import jax.numpy as jnp
from jax import lax
from jax.experimental import pallas as pl
from jax.experimental.pallas import tpu as pltpu

F32 = jnp.float32
BF16 = jnp.bfloat16

NORM_EPS = 1e-6
L2_EPS = 1e-6
CONV_K = 4
SUBLANES = 8
HALO = 8
CHUNK = 64
INV_BASE = 16
SEQ_TILE = 256
SSD_TILE = 1024
SSM_HEAD_DIM = 64
SSM_N_GROUPS = 8
SSM_D_STATE = 128
GDN_HEAD_DIM = 128
GDN_V_PER_QK = 2
GDN_QK_PER_STEP = 8
SMALL_W = 128
SPLIT_PARTS = 3
INPROJ_ROWS = 256
INPROJ_W_PARTS = 4
VMEM_LIMIT = 56 * 1024 * 1024


def _cparams(sem):
    return pltpu.CompilerParams(dimension_semantics=sem, vmem_limit_bytes=VMEM_LIMIT)


def _bdot(a, b):
    return jnp.dot(a.astype(BF16), b.astype(BF16), preferred_element_type=F32)


def _bdot_nt(a, b):
    return lax.dot_general(a.astype(BF16), b.astype(BF16), (((1,), (1,)), ((), ())),
                           preferred_element_type=F32)


def _bdot_tn(a, b):
    return lax.dot_general(a.astype(BF16), b.astype(BF16), (((0,), (0,)), ((), ())),
                           preferred_element_type=F32)


def _split3(x):
    hi = x.astype(BF16)
    r1 = x - hi.astype(F32)
    mid = r1.astype(BF16)
    lo = (r1 - mid.astype(F32)).astype(BF16)
    return hi, mid, lo


def _dot01_left(m01, x):
    hi, mid, lo = _split3(x)
    d = lambda p: jnp.dot(m01, p, preferred_element_type=F32)
    return (d(lo) + d(mid)) + d(hi)


def _dot01_right(x, m01_x3):
    return jnp.dot(jnp.concatenate(_split3(x), axis=1), m01_x3, preferred_element_type=F32)


def _dot01_tn(x, m01):
    hi, mid, lo = _split3(x)
    d = lambda p: lax.dot_general(p, m01, (((0,), (0,)), ((), ())), preferred_element_type=F32)
    return (d(lo) + d(mid)) + d(hi)


def _sigmoid(x):
    return 1.0 / (1.0 + jnp.exp(-x))


def _silu(x):
    return x * _sigmoid(x)


def _silu_tanh(x):
    h = 0.5 * x
    return h * jnp.tanh(h) + h


def _softplus(x):
    return jnp.maximum(x, 0.0) + jnp.log(1.0 + jnp.exp(-jnp.abs(x)))


def _rms(x, eps=NORM_EPS):
    return x * lax.rsqrt(jnp.mean(x * x, axis=-1, keepdims=True) + eps)


def _conv_silu(stage_ref, cur, w_ref, bias, rows):
    stage_ref[HALO:HALO + rows, :] = cur
    acc = None
    for k in reversed(range(CONV_K)):
        start = HALO - (CONV_K - 1) + k
        term = stage_ref[start:start + rows, :] * w_ref[k:k + 1, :]
        acc = term if acc is None else acc + term
    if bias is not None:
        acc = acc + bias
    stage_ref[0:HALO, :] = cur[rows - HALO:rows, :]
    return _silu_tanh(acc)


def _ada_kernel(c_ref, w_ref, b_ref, o_ref):
    c_act = _silu(c_ref[...])
    o_ref[...] = _bdot(c_act, w_ref[...]) + b_ref[...]


def _adaln(c, w_ada, b_ada):
    bsz, d = c.shape
    n = w_ada.shape[1]
    tn = 1024
    return pl.pallas_call(
        _ada_kernel,
        grid=(n // tn,),
        in_specs=[pl.BlockSpec((bsz, d), lambda j: (0, 0)),
                  pl.BlockSpec((d, tn), lambda j: (0, j)),
                  pl.BlockSpec((1, tn), lambda j: (0, j))],
        out_specs=pl.BlockSpec((bsz, tn), lambda j: (0, j)),
        out_shape=jax.ShapeDtypeStruct((bsz, n), F32),
        compiler_params=_cparams(("arbitrary",)),
        name="adaln",
    )(c, w_ada, b_ada.reshape(1, n))


def _norm_mod_kernel(x_ref, w_ref, shift_ref, scale_ref, o_ref):
    y = _rms(x_ref[...]) * w_ref[...]
    o_ref[...] = (y * (1.0 + scale_ref[...]) + shift_ref[...]).astype(o_ref.dtype)


def _norm_mod(x, w, mod3, shift_idx, scale_idx):
    bsz, seq, d = x.shape
    tm = 512
    return pl.pallas_call(
        _norm_mod_kernel,
        grid=(bsz, seq // tm),
        in_specs=[pl.BlockSpec((None, tm, d), lambda b, i: (b, i, 0)),
                  pl.BlockSpec((1, d), lambda b, i: (0, 0)),
                  pl.BlockSpec((None, 1, d), lambda b, i: (b, 0, shift_idx)),
                  pl.BlockSpec((None, 1, d), lambda b, i: (b, 0, scale_idx))],
        out_specs=pl.BlockSpec((None, tm, d), lambda b, i: (b, i, 0)),
        out_shape=jax.ShapeDtypeStruct((bsz, seq, d), BF16),
        compiler_params=_cparams(("arbitrary", "arbitrary")),
        name="norm_mod",
    )(x, w.reshape(1, d), mod3, mod3)


def _mm_kernel(x_ref, w_ref, o_ref):
    w = w_ref[...].astype(x_ref.dtype)
    o_ref[...] = jnp.dot(x_ref[...], w, preferred_element_type=F32).astype(o_ref.dtype)


def _matmul(x, w, tm, tn, out_dtype, name):
    m, k = x.shape
    n = w.shape[1]
    return pl.pallas_call(
        _mm_kernel,
        grid=(n // tn, m // tm),
        in_specs=[pl.BlockSpec((tm, k), lambda j, i: (i, 0)),
                  pl.BlockSpec((k, tn), lambda j, i: (0, j))],
        out_specs=pl.BlockSpec((tm, tn), lambda j, i: (i, j)),
        out_shape=jax.ShapeDtypeStruct((m, n), out_dtype),
        compiler_params=_cparams(("arbitrary", "arbitrary")),
        name=name,
    )(x, w)


def _inproj_kernel(x_ref, *refs, n_shift1, n_shift2, shift1, shift2):
    wa_refs = refs[:INPROJ_W_PARTS]
    wb_ref, o_ref, wbf_ref = refs[INPROJ_W_PARTS:]
    n = pl.program_id(0)
    kdim = x_ref.shape[1]

    @pl.when(pl.program_id(1) == 0)
    def _():
        def repack(shift):
            part = kdim // INPROJ_W_PARTS
            for r0 in range(0, kdim, INPROJ_ROWS):
                rs = slice(r0, r0 + INPROJ_ROWS)
                wa_ref = wa_refs[r0 // part]
                ra = slice(r0 % part, r0 % part + INPROJ_ROWS)
                if shift == 0:
                    w = wa_ref[ra, :]
                else:
                    w = jnp.concatenate([wa_ref[ra, shift:], wb_ref[rs, :shift]], axis=1)
                wbf_ref[rs, :] = w.astype(BF16)

        @pl.when(n < n_shift1)
        def _():
            repack(0)

        @pl.when((n >= n_shift1) & (n < n_shift2))
        def _():
            repack(shift1)

        @pl.when(n >= n_shift2)
        def _():
            repack(shift2)

    o_ref[...] = jnp.dot(x_ref[...], wbf_ref[...], preferred_element_type=F32)


def _inproj(h, w_in, layer, offs, n_main, shift1, shift2):
    m, k = h.shape
    tm, tn = 1024, 1024
    segs = (offs["gate_ssm"], n_main, offs["gdn_q"])
    assert all(s % tn == 0 for s in segs) and shift2 == SMALL_W and 0 < shift1 < SMALL_W
    n_tiles = n_main // tn
    parts = INPROJ_W_PARTS
    assert m // tm > parts + 1 and k % (parts * INPROJ_ROWS) == 0
    tile = lambda j, i, after: jnp.minimum(j + (i >= after).astype(jnp.int32), n_tiles - 1)
    wa_spec = lambda p: pl.BlockSpec((None, k // parts, tn), lambda j, i: (layer, p, tile(j, i, p + 1)))
    return pl.pallas_call(
        functools.partial(_inproj_kernel, n_shift1=offs["gdn_q"] // tn, n_shift2=offs["gate_ssm"] // tn,
                          shift1=shift1, shift2=shift2),
        grid=(n_main // tn, m // tm),
        in_specs=[pl.BlockSpec((tm, k), lambda j, i: (i, 0))]
                 + [wa_spec(p) for p in range(parts)]
                 + [pl.BlockSpec((None, k, SMALL_W),
                                 lambda j, i: (layer, 0, (tile(j, i, parts + 1) + 1) * (tn // SMALL_W)))],
        out_specs=pl.BlockSpec((tm, tn), lambda j, i: (i, j)),
        out_shape=jax.ShapeDtypeStruct((m, n_main), F32),
        scratch_shapes=[pltpu.VMEM((k, tn), BF16)],
        compiler_params=_cparams(("arbitrary", "arbitrary")),
        name="in_proj",
    )(h, *([w_in] * (parts + 1)))


def _prep_kernel(sm_ref, bias_ref, alog_ref, tril_ref, eye_ref, p1_ref, p2_ref, p2t_ref, *, n_dt, n_beta):
    sm = sm_ref[...]
    lane = lax.broadcasted_iota(jnp.int32, sm.shape, 1)
    is_dt = lane < n_dt
    is_beta = (lane >= n_dt) & (lane < n_dt + n_beta)
    sp = _softplus(sm + bias_ref[...])
    rate = -jnp.exp(alog_ref[...]) * sp
    p1_ref[...] = jnp.where(is_beta, _sigmoid(sm), jnp.where(is_dt, sp, rate))
    cum = _dot01_left(tril_ref[...], jnp.where(is_beta, 0.0, rate))
    p2_ref[...] = cum
    p2t_ref[...] = _dot01_tn(cum, eye_ref[...])


def _prep(small, bias, alog, n_dt, n_beta):
    t = small.shape[0]
    ts = SEQ_TILE
    r = jnp.arange(ts)
    tril = ((r[:, None] // CHUNK == r[None, :] // CHUNK) & (r[:, None] >= r[None, :])).astype(BF16)
    eye = (r[:, None] == r[None, :]).astype(BF16)
    return pl.pallas_call(
        functools.partial(_prep_kernel, n_dt=n_dt, n_beta=n_beta),
        grid=(t // ts,),
        in_specs=[pl.BlockSpec((ts, SMALL_W), lambda i: (i, 0)),
                  pl.BlockSpec((1, SMALL_W), lambda i: (0, 0)),
                  pl.BlockSpec((1, SMALL_W), lambda i: (0, 0)),
                  pl.BlockSpec((ts, ts), lambda i: (0, 0)),
                  pl.BlockSpec((ts, ts), lambda i: (0, 0))],
        out_specs=[pl.BlockSpec((ts, SMALL_W), lambda i: (i, 0)),
                   pl.BlockSpec((ts, SMALL_W), lambda i: (i, 0)),
                   pl.BlockSpec((SMALL_W, ts), lambda i: (0, i))],
        out_shape=[jax.ShapeDtypeStruct((t, SMALL_W), F32),
                   jax.ShapeDtypeStruct((t, SMALL_W), F32),
                   jax.ShapeDtypeStruct((SMALL_W, t), F32)],
        compiler_params=_cparams(("arbitrary",)),
        name="head_scalars",
    )(small, bias, alog, tril, eye)


def _ssd_kernel(z_ref, x_ref, b_ref, c_ref, p1_ref, p2_ref, arow_ref, e_ref, causal_ref, headbd_ref,
                xpar_ref, bpar_ref, cpar_ref, o_ref,
                xs_ref, bs_ref, cs_ref, st_ref):
    rows = x_ref.shape[0]
    gw = x_ref.shape[1]
    n_heads = gw // SSM_HEAD_DIM
    bias_row = slice(CONV_K, CONV_K + 1)
    dsk_row = slice(CONV_K + 1, CONV_K + 2)
    nw_row = slice(CONV_K + 2, CONV_K + 3)

    @pl.when(pl.program_id(2) == 0)
    def _():
        xs_ref[0:HALO, :] = jnp.zeros((HALO, gw), F32)
        bs_ref[0:HALO, :] = jnp.zeros((HALO, SSM_D_STATE), F32)
        cs_ref[0:HALO, :] = jnp.zeros((HALO, SSM_D_STATE), F32)
        st_ref[...] = jnp.zeros_like(st_ref)

    xa = _conv_silu(xs_ref, x_ref[...], xpar_ref, xpar_ref[bias_row, :], rows)
    ba = _conv_silu(bs_ref, b_ref[...], bpar_ref, bpar_ref[bias_row, :], rows)
    ca = _conv_silu(cs_ref, c_ref[...], cpar_ref, cpar_ref[bias_row, :], rows)

    e = e_ref[...]
    dt = _dot01_right(p1_ref[...], e)
    a_cum = _dot01_right(p2_ref[...], e)
    xdt = xa * dt
    ea = jnp.exp(a_cum)
    head_bd = headbd_ref[...]

    for ci in range(rows // CHUNK):
        sl = slice(ci * CHUNK, (ci + 1) * CHUNK)
        b_c, c_c = ba[sl], ca[sl]
        ac = a_cum[sl]
        a_last = ac[CHUNK - 1:CHUNK, :]
        xdt_c = xdt[sl]
        a_row = jnp.concatenate([arow_ref[h:h + 1, sl] for h in range(n_heads)], axis=1)
        decay = jnp.exp(jnp.minimum(ac - a_row, 0.0)) * causal_ref[...]
        cb = _bdot_nt(c_c, jnp.concatenate([b_c] * n_heads, axis=0))
        scores = cb * decay
        x_bd = jnp.concatenate([xdt_c.astype(BF16)] * n_heads, axis=0) * head_bd
        y_diag = jnp.dot(scores.astype(BF16), x_bd, preferred_element_type=F32)
        state = st_ref[...]
        y_off = _bdot(c_c, state) * ea[sl]
        xw = xdt_c * jnp.exp(a_last - ac)
        st_ref[...] = state * jnp.exp(a_last) + _bdot_tn(b_c, xw)
        y = y_diag + y_off + xpar_ref[dsk_row, :] * xa[sl]
        y = y * _silu_tanh(z_ref[sl, :])
        o_ref[sl, :] = (_rms(y) * xpar_ref[nw_row, :]).astype(o_ref.dtype)


def _ssd(proj, p1, p2, p2t, e_sel, conv_w, conv_b, d_skip, norm_w, bsz, seq, offs):
    d_in = norm_w.shape[0]
    gw = d_in // SSM_N_GROUPS
    hpg = gw // SSM_HEAD_DIM
    n = SSM_D_STATE
    ts = SSD_TILE
    nt = seq // ts
    zrow = jnp.zeros_like(conv_b)
    xtra = lambda v: jnp.concatenate([v, jnp.zeros((conv_b.shape[0] - d_in,), F32)])
    par = jnp.concatenate([conv_w, conv_b[None, :], xtra(jnp.repeat(d_skip, SSM_HEAD_DIM))[None, :],
                           xtra(norm_w)[None, :], zrow[None, :]], axis=0)
    assert par.shape[0] == SUBLANES
    z0 = offs["ssm_z"] // gw
    x0 = offs["ssm_x"] // gw
    b0 = offs["ssm_b"] // n
    c0 = offs["ssm_c"] // n
    cb0 = d_in // n
    cc0 = (d_in + SSM_N_GROUPS * n) // n
    row = lambda b, g, i: b * nt + i
    r = jnp.arange(CHUNK)[:, None]
    l = jnp.arange(gw)[None, :]
    causal = (r >= l % SSM_HEAD_DIM).astype(F32)
    k = jnp.arange(gw)
    head_bd = (k[:, None] // SSM_HEAD_DIM == k[None, :] // SSM_HEAD_DIM).astype(BF16)
    return pl.pallas_call(
        _ssd_kernel,
        grid=(bsz, SSM_N_GROUPS, nt),
        in_specs=[
            pl.BlockSpec((ts, gw), lambda b, g, i: (row(b, g, i), z0 + g)),
            pl.BlockSpec((ts, gw), lambda b, g, i: (row(b, g, i), x0 + g)),
            pl.BlockSpec((ts, n), lambda b, g, i: (row(b, g, i), b0 + g)),
            pl.BlockSpec((ts, n), lambda b, g, i: (row(b, g, i), c0 + g)),
            pl.BlockSpec((ts, SMALL_W), lambda b, g, i: (row(b, g, i), 0)),
            pl.BlockSpec((ts, SMALL_W), lambda b, g, i: (row(b, g, i), 0)),
            pl.BlockSpec((hpg, ts), lambda b, g, i: (g, row(b, g, i))),
            pl.BlockSpec((None, SPLIT_PARTS * SMALL_W, gw), lambda b, g, i: (g, 0, 0)),
            pl.BlockSpec((CHUNK, gw), lambda b, g, i: (0, 0)),
            pl.BlockSpec((gw, gw), lambda b, g, i: (0, 0)),
            pl.BlockSpec((SUBLANES, gw), lambda b, g, i: (0, g)),
            pl.BlockSpec((SUBLANES, n), lambda b, g, i: (0, cb0 + g)),
            pl.BlockSpec((SUBLANES, n), lambda b, g, i: (0, cc0 + g)),
        ],
        out_specs=pl.BlockSpec((ts, gw), lambda b, g, i: (row(b, g, i), g)),
        out_shape=jax.ShapeDtypeStruct((bsz * seq, d_in), BF16),
        scratch_shapes=[pltpu.VMEM((ts + HALO, gw), F32),
                        pltpu.VMEM((ts + HALO, n), F32),
                        pltpu.VMEM((ts + HALO, n), F32),
                        pltpu.VMEM((n, gw), F32)],
        compiler_params=_cparams(("arbitrary", "arbitrary", "arbitrary")),
        name="ssd_mixer",
    )(proj, proj, proj, proj, p1, p2, p2t, e_sel, causal, head_bd, par, par, par)


def _gdn_kernel(q_ref, k_ref, v_ref, z_ref, p1_ref, p2_ref, grow_ref, eb_ref, eg_ref,
                incl_ref, eye_ref, nbase_ref, e1_ref, e2_ref,
                cwq_ref, cwk_ref, cwv_ref, nw_ref, o_ref,
                qs_ref, ks_ref, vs_ref, st_ref):
    rows = q_ref.shape[0]
    dk = dv = GDN_HEAD_DIM
    n_qk = q_ref.shape[1] // dk
    n_v = v_ref.shape[1] // dv
    assert rows == 2 * dk

    @pl.when(pl.program_id(2) == 0)
    def _():
        qs_ref[0:HALO, :] = jnp.zeros((HALO, n_qk * dk), F32)
        ks_ref[0:HALO, :] = jnp.zeros((HALO, n_qk * dk), F32)
        vs_ref[0:HALO, :] = jnp.zeros((HALO, n_v * dv), F32)
        st_ref[...] = jnp.zeros_like(st_ref)

    q_all = _conv_silu(qs_ref, q_ref[...], cwq_ref, None, rows)
    k_all = _conv_silu(ks_ref, k_ref[...], cwk_ref, None, rows)
    v_all = _conv_silu(vs_ref, v_ref[...], cwv_ref, None, rows)
    beta = _dot01_right(p1_ref[...], eb_ref[...])
    gam = _dot01_right(p2_ref[...], eg_ref[...])
    e_gam = jnp.exp(gam)
    incl_bf = incl_ref[...]
    g_off = (pl.program_id(1) * n_v) % SUBLANES

    qs, ks, kks, qks = [], [], [], []
    for j in range(n_qk):
        js = slice(j * dk, (j + 1) * dk)
        q = q_all[:, js]
        k = k_all[:, js]
        q = q * (lax.rsqrt(jnp.sum(q * q, axis=-1, keepdims=True) + L2_EPS) * (dk ** -0.5))
        k = k * lax.rsqrt(jnp.sum(k * k, axis=-1, keepdims=True) + L2_EPS)
        qs.append(q)
        ks.append(k)
        kks.append(_bdot_nt(k, k))
        qks.append(_bdot_nt(q, k))

    heads = range(n_v)
    eye = eye_ref[...]
    t_inv, sq, e_lo, rhs, qkm, gbs = [], [], [], [], [], []
    for h in heads:
        j = h // GDN_V_PER_QK
        hs = slice(h * dv, (h + 1) * dv)
        gb = gam[:, hs]
        bb = beta[:, hs]
        seg = jnp.concatenate([gb, gb], axis=1) - grow_ref[pl.ds(g_off + h, 1), :]
        dec = jnp.exp(jnp.minimum(seg, 0.0))
        a = kks[j] * dec * jnp.concatenate([bb, bb], axis=1)
        b0 = a * nbase_ref[...]
        sq.append(b0)
        t_inv.append(eye + b0)
        a_bf = a.astype(BF16)
        e_lo.append([a_bf * m_ref[...] for m_ref in (e1_ref, e2_ref)])
        rhs.append(jnp.concatenate([v_all[:, hs] * bb, ks[j] * (bb * e_gam[:, hs])], axis=1))
        qkm.append((qks[j] * dec).astype(BF16) * incl_bf)
        gbs.append(gb)

    n_sq = INV_BASE.bit_length() - 2
    for h in heads:
        sq[h] = _bdot(sq[h], sq[h])
    for lvl in range(n_sq):
        for h in heads:
            if lvl == n_sq - 1:
                t_inv[h] = t_inv[h] + _bdot(t_inv[h], sq[h])
            else:
                res = _bdot(jnp.concatenate([t_inv[h], sq[h]], axis=0), sq[h])
                t_inv[h] = t_inv[h] + res[:rows]
                sq[h] = res[rows:]
    for lvl in range(len(e_lo[0])):
        for h in heads:
            t_inv[h] = t_inv[h] - _bdot(t_inv[h], _bdot(e_lo[h][lvl], t_inv[h]))
    xs = [_bdot(t_inv[h], rhs[h]) for h in heads]

    qd = [qs[h // GDN_V_PER_QK] * e_gam[:, h * dv:(h + 1) * dv] for h in heads]
    qs_parts = [[] for _ in heads]
    vn_parts = [[] for _ in heads]
    for ci in range(rows // CHUNK):
        sl = slice(ci * CHUNK, (ci + 1) * CHUNK)
        for h in heads:
            j = h // GDN_V_PER_QK
            u_c, w_c = xs[h][sl, :dv], xs[h][sl, dv:]
            gb = gbs[h]
            gl = gb[ci * CHUNK + CHUNK - 1:(ci + 1) * CHUNK, :]
            state = st_ref[h]
            wq = _bdot(jnp.concatenate([w_c, qd[h][sl]], axis=0), state)
            v_new = u_c - wq[:CHUNK]
            k_tail = ks[j][sl] * jnp.exp(gl - gb[sl])
            st_ref[h] = state * jnp.exp(gl) + _bdot_tn(k_tail, v_new)
            qs_parts[h].append(wq[CHUNK:])
            vn_parts[h].append(v_new)
    z = z_ref[...]
    for h in heads:
        hs = slice(h * dv, (h + 1) * dv)
        o = jnp.concatenate(qs_parts[h], axis=0) + jnp.dot(
            qkm[h], jnp.concatenate(vn_parts[h], axis=0).astype(BF16), preferred_element_type=F32)
        o = _rms(o) * nw_ref[...] * _silu_tanh(z[:, hs])
        o_ref[:, hs] = o.astype(o_ref.dtype)


def _gdn(proj, p1, p2, p2t, e_beta, e_gam, conv_w, norm_w, bsz, seq, offs, n_qk, g_row0):
    dk = GDN_HEAD_DIM
    qb = GDN_QK_PER_STEP
    vb = qb * GDN_V_PER_QK
    qw = qb * dk
    vw = vb * GDN_HEAD_DIM
    ts = SEQ_TILE
    nt = seq // ts
    q0 = offs["gdn_q"] // qw
    k0 = offs["gdn_k"] // qw
    v0 = offs["gdn_v"] // vw
    z0 = offs["gdn_z"] // vw
    ck0 = (n_qk * dk) // qw
    cv0 = (2 * n_qk * dk) // vw
    assert g_row0 % SUBLANES == 0 and (SUBLANES % vb == 0 or vb % SUBLANES == 0)
    row = lambda b, g, i: b * nt + i
    r = jnp.arange(ts)
    same = r[:, None] // CHUNK == r[None, :] // CHUNK
    lower = r[:, None] > r[None, :]
    incl = (same & (r[:, None] >= r[None, :])).astype(BF16)
    eye = (r[:, None] == r[None, :]).astype(F32)
    blk = lambda size: r[:, None] // size == r[None, :] // size
    nbase = -(blk(INV_BASE) & lower).astype(F32)
    e_masks = []
    size = INV_BASE
    while size < CHUNK:
        e_masks.append((blk(2 * size) & ~blk(size) & lower).astype(BF16))
        size *= 2
    assert len(e_masks) == 2
    return pl.pallas_call(
        _gdn_kernel,
        grid=(bsz, n_qk // qb, nt),
        in_specs=[
            pl.BlockSpec((ts, qw), lambda b, g, i: (row(b, g, i), q0 + g)),
            pl.BlockSpec((ts, qw), lambda b, g, i: (row(b, g, i), k0 + g)),
            pl.BlockSpec((ts, vw), lambda b, g, i: (row(b, g, i), v0 + g)),
            pl.BlockSpec((ts, vw), lambda b, g, i: (row(b, g, i), z0 + g)),
            pl.BlockSpec((ts, SMALL_W), lambda b, g, i: (row(b, g, i), 0)),
            pl.BlockSpec((ts, SMALL_W), lambda b, g, i: (row(b, g, i), 0)),
            pl.BlockSpec((max(vb, SUBLANES), ts),
                         lambda b, g, i: ((g_row0 + g * vb) // max(vb, SUBLANES), row(b, g, i))),
            pl.BlockSpec((None, SPLIT_PARTS * SMALL_W, vw), lambda b, g, i: (g, 0, 0)),
            pl.BlockSpec((None, SPLIT_PARTS * SMALL_W, vw), lambda b, g, i: (g, 0, 0)),
            pl.BlockSpec((ts, ts), lambda b, g, i: (0, 0)),
            pl.BlockSpec((ts, ts), lambda b, g, i: (0, 0)),
            pl.BlockSpec((ts, ts), lambda b, g, i: (0, 0)),
            pl.BlockSpec((ts, ts), lambda b, g, i: (0, 0)),
            pl.BlockSpec((ts, ts), lambda b, g, i: (0, 0)),
            pl.BlockSpec((CONV_K, qw), lambda b, g, i: (0, g)),
            pl.BlockSpec((CONV_K, qw), lambda b, g, i: (0, ck0 + g)),
            pl.BlockSpec((CONV_K, vw), lambda b, g, i: (0, cv0 + g)),
            pl.BlockSpec((1, dk), lambda b, g, i: (0, 0)),
        ],
        out_specs=pl.BlockSpec((ts, vw), lambda b, g, i: (row(b, g, i), g)),
        out_shape=jax.ShapeDtypeStruct((bsz * seq, n_qk * GDN_V_PER_QK * GDN_HEAD_DIM), BF16),
        scratch_shapes=[pltpu.VMEM((ts + HALO, qw), F32),
                        pltpu.VMEM((ts + HALO, qw), F32),
                        pltpu.VMEM((ts + HALO, vw), F32),
                        pltpu.VMEM((vb, dk, GDN_HEAD_DIM), F32)],
        compiler_params=_cparams(("arbitrary", "arbitrary", "arbitrary")),
        name="gdn_mixer",
    )(proj, proj, proj, proj, p1, p2, p2t, e_beta, e_gam, incl, eye, nbase, e_masks[0], e_masks[1],
      conv_w, conv_w, conv_w, norm_w)


def _merge_kernel(ys_ref, yg_ref, gs_ref, gg_ref, ws_ref, wg_ref, o_ref):
    a = jnp.dot(ys_ref[...], ws_ref[...], preferred_element_type=F32)
    b = jnp.dot(yg_ref[...], wg_ref[...], preferred_element_type=F32)
    o_ref[...] = (_sigmoid(gs_ref[...]) * a + _sigmoid(gg_ref[...]) * b).astype(o_ref.dtype)


def _merge(y_ssm, y_gdn, proj, w_ssm, w_gdn, offs):
    m, k = y_ssm.shape
    n = w_ssm.shape[1]
    tm, tn = 512, 512
    gs0 = offs["gate_ssm"] // tn
    gg0 = offs["gate_gdn"] // tn
    return pl.pallas_call(
        _merge_kernel,
        grid=(n // tn, m // tm),
        in_specs=[pl.BlockSpec((tm, k), lambda j, i: (i, 0)),
                  pl.BlockSpec((tm, k), lambda j, i: (i, 0)),
                  pl.BlockSpec((tm, tn), lambda j, i: (i, gs0 + j)),
                  pl.BlockSpec((tm, tn), lambda j, i: (i, gg0 + j)),
                  pl.BlockSpec((k, tn), lambda j, i: (0, j)),
                  pl.BlockSpec((k, tn), lambda j, i: (0, j))],
        out_specs=pl.BlockSpec((tm, tn), lambda j, i: (i, j)),
        out_shape=jax.ShapeDtypeStruct((m, n), BF16),
        compiler_params=_cparams(("arbitrary", "arbitrary")),
        name="merge_proj",
    )(y_ssm, y_gdn, proj, proj, w_ssm, w_gdn)


def _mixout_kernel(m_ref, wo_ref, x_ref, gate_ref, nw_ref, shift_ref, scale_ref, x1_ref, h2_ref):
    mix = jnp.dot(m_ref[...], wo_ref[...], preferred_element_type=F32)
    x1 = x_ref[...] + gate_ref[...] * mix
    x1_ref[...] = x1
    y = _rms(x1) * nw_ref[...]
    h2_ref[...] = (y * (1.0 + scale_ref[...]) + shift_ref[...]).astype(h2_ref.dtype)


def _mixout(merged, w_o, x, mod3, norm_w, gate_idx, shift_idx, scale_idx):
    bsz, seq, d = x.shape
    tm = 512
    nt = seq // tm
    return pl.pallas_call(
        _mixout_kernel,
        grid=(bsz, nt),
        in_specs=[pl.BlockSpec((tm, d), lambda b, i: (b * nt + i, 0)),
                  pl.BlockSpec((d, d), lambda b, i: (0, 0)),
                  pl.BlockSpec((None, tm, d), lambda b, i: (b, i, 0)),
                  pl.BlockSpec((None, 1, d), lambda b, i: (b, 0, gate_idx)),
                  pl.BlockSpec((1, d), lambda b, i: (0, 0)),
                  pl.BlockSpec((None, 1, d), lambda b, i: (b, 0, shift_idx)),
                  pl.BlockSpec((None, 1, d), lambda b, i: (b, 0, scale_idx))],
        out_specs=[pl.BlockSpec((None, tm, d), lambda b, i: (b, i, 0)),
                   pl.BlockSpec((None, tm, d), lambda b, i: (b, i, 0))],
        out_shape=[jax.ShapeDtypeStruct((bsz, seq, d), F32),
                   jax.ShapeDtypeStruct((bsz, seq, d), BF16)],
        compiler_params=_cparams(("arbitrary", "arbitrary")),
        name="mix_out",
    )(merged, w_o, x, mod3, norm_w.reshape(1, d), mod3, mod3)


def _gateup_kernel(h_ref, wg_ref, wu_ref, o_ref, wgb_ref, wub_ref):
    @pl.when(pl.program_id(1) == 0)
    def _():
        wgb_ref[...] = wg_ref[...].astype(BF16)
        wub_ref[...] = wu_ref[...].astype(BF16)

    h = h_ref[...]
    g = jnp.dot(h, wgb_ref[...], preferred_element_type=F32)
    u = jnp.dot(h, wub_ref[...], preferred_element_type=F32)
    o_ref[...] = (_silu(g) * u).astype(o_ref.dtype)


def _gateup(h, w_gate_up, layer):
    m, k = h.shape
    hid = w_gate_up.shape[2] // 2
    tm, tn = 1024, 512
    up0 = hid // tn
    assert m // tm > 2
    tile = lambda j, i, after: jnp.minimum(j + (i >= after).astype(jnp.int32), up0 - 1)
    return pl.pallas_call(
        _gateup_kernel,
        grid=(hid // tn, m // tm),
        in_specs=[pl.BlockSpec((tm, k), lambda j, i: (i, 0)),
                  pl.BlockSpec((None, k, tn), lambda j, i: (layer, 0, tile(j, i, 1))),
                  pl.BlockSpec((None, k, tn), lambda j, i: (layer, 0, up0 + tile(j, i, 2)))],
        out_specs=pl.BlockSpec((tm, tn), lambda j, i: (i, j)),
        out_shape=jax.ShapeDtypeStruct((m, hid), BF16),
        scratch_shapes=[pltpu.VMEM((k, tn), BF16), pltpu.VMEM((k, tn), BF16)],
        compiler_params=_cparams(("arbitrary", "arbitrary")),
        name="ffn_gate_up",
    )(h, w_gate_up, w_gate_up)


def _down_kernel(a_ref, wd_ref, x1_ref, gate_ref, fw_ref, o_ref, *, final):
    ffn = jnp.dot(a_ref[...], wd_ref[...], preferred_element_type=F32)
    x2 = x1_ref[...] + gate_ref[...] * ffn
    if final:
        x2 = _rms(x2) * fw_ref[...]
    o_ref[...] = x2


def _down(act, w_down, x1, mod3, final_w, gate_idx, final):
    bsz, seq, d = x1.shape
    hid = act.shape[1]
    tm = 512
    nt = seq // tm
    return pl.pallas_call(
        functools.partial(_down_kernel, final=final),
        grid=(bsz, nt),
        in_specs=[pl.BlockSpec((tm, hid), lambda b, i: (b * nt + i, 0)),
                  pl.BlockSpec((hid, d), lambda b, i: (0, 0), pipeline_mode=pl.Buffered(1)),
                  pl.BlockSpec((None, tm, d), lambda b, i: (b, i, 0)),
                  pl.BlockSpec((None, 1, d), lambda b, i: (b, 0, gate_idx)),
                  pl.BlockSpec((1, d), lambda b, i: (0, 0))],
        out_specs=pl.BlockSpec((None, tm, d), lambda b, i: (b, i, 0)),
        out_shape=jax.ShapeDtypeStruct((bsz, seq, d), F32),
        compiler_params=_cparams(("arbitrary", "arbitrary")),
        name="ffn_down",
    )(act, w_down, x1, mod3, final_w.reshape(1, d))


def _selector(n_groups, src0, per_group, width, rep):
    k = jnp.arange(SPLIT_PARTS * SMALL_W)[None, :, None] % SMALL_W
    g = jnp.arange(n_groups)[:, None, None]
    c = jnp.arange(width)[None, None, :]
    return (k == src0 + g * per_group + c // rep).astype(BF16)


def kernel(x, c, w_ada, b_ada, norm_mix_w, w_in, ssm_conv_w, ssm_conv_b, ssm_dt_bias, ssm_a_log, ssm_d_skip, ssm_norm_w, gdn_conv_w, gdn_a_log, gdn_dt_bias, gdn_norm_w, w_ssm_proj, w_gdn_proj, w_o, norm_ffn_w, w_gate_up, w_down, final_norm_w):
    bsz, seq, d = x.shape
    depth = w_ada.shape[0]
    d_in = ssm_norm_w.shape[1]
    n_ssm_heads = ssm_dt_bias.shape[1]
    gn = SSM_N_GROUPS * SSM_D_STATE
    n_v = gdn_a_log.shape[1]
    n_qk = n_v // GDN_V_PER_QK
    qk_dim = n_qk * GDN_HEAD_DIM
    v_dim = n_v * GDN_HEAD_DIM
    assert n_ssm_heads + 2 * n_v == SMALL_W and seq % SEQ_TILE == 0 and seq % SSD_TILE == 0

    sizes = (d_in, d_in + 2 * gn, n_ssm_heads, 2 * qk_dim + v_dim, v_dim, n_v, n_v, d, d)
    starts = [0]
    for s in sizes:
        starts.append(starts[-1] + s)
    (s_z, s_xbc, s_dt, s_qkv, s_gz, s_beta, s_a, s_gs, s_gg) = starts[:-1]
    offs = {"ssm_z": 0, "ssm_x": d_in, "ssm_b": 2 * d_in, "ssm_c": 2 * d_in + gn}
    o_qkv = 2 * d_in + 2 * gn
    offs.update({"gdn_q": o_qkv, "gdn_k": o_qkv + qk_dim, "gdn_v": o_qkv + 2 * qk_dim})
    offs["gdn_z"] = o_qkv + 2 * qk_dim + v_dim
    offs["gate_ssm"] = offs["gdn_z"] + v_dim
    offs["gate_gdn"] = offs["gate_ssm"] + d

    vb = GDN_QK_PER_STEP * GDN_V_PER_QK
    e_dt = _selector(SSM_N_GROUPS, 0, n_ssm_heads // SSM_N_GROUPS, d_in // SSM_N_GROUPS, SSM_HEAD_DIM)
    e_beta = _selector(n_v // vb, n_ssm_heads, vb, vb * GDN_HEAD_DIM, GDN_HEAD_DIM)
    e_gam = _selector(n_v // vb, n_ssm_heads + n_v, vb, vb * GDN_HEAD_DIM, GDN_HEAD_DIM)

    out = x
    for layer in range(depth):
        wl = w_in[layer]
        w_small = jnp.concatenate([wl[:, s_dt:s_qkv], wl[:, s_beta:s_gs]], axis=1)

        mod = _adaln(c, w_ada[layer], b_ada[layer])
        mod3 = mod.reshape(bsz, 1, 6 * d)
        h = _norm_mod(out, norm_mix_w[layer], mod3, 0, 1).reshape(bsz * seq, d)
        proj = _inproj(h, w_in, layer, offs, offs["gate_gdn"] + d, s_qkv - s_dt, s_gs - s_gz - v_dim + s_qkv - s_dt)
        small = _matmul(h, w_small, 1024, SMALL_W, F32, "in_proj_small")

        zeros_b = jnp.zeros((n_v,), F32)
        bias = jnp.concatenate([ssm_dt_bias[layer], zeros_b, gdn_dt_bias[layer]])[None, :]
        alog = jnp.concatenate([ssm_a_log[layer], zeros_b, gdn_a_log[layer]])[None, :]
        p1, p2, p2t = _prep(small, bias, alog, n_ssm_heads, n_v)

        y_ssm = _ssd(proj, p1, p2, p2t, e_dt, ssm_conv_w[layer], ssm_conv_b[layer], ssm_d_skip[layer],
                     ssm_norm_w[layer], bsz, seq, offs)
        y_gdn = _gdn(proj, p1, p2, p2t, e_beta, e_gam, gdn_conv_w[layer], gdn_norm_w[layer][None, :],
                     bsz, seq, offs, n_qk, n_ssm_heads + n_v)

        merged = _merge(y_ssm, y_gdn, proj, w_ssm_proj[layer].astype(BF16), w_gdn_proj[layer].astype(BF16), offs)
        x1, h2 = _mixout(merged, w_o[layer].astype(BF16), out, mod3, norm_ffn_w[layer], 2, 3, 4)
        act = _gateup(h2.reshape(bsz * seq, d), w_gate_up, layer)
        out = _down(act, w_down[layer].astype(BF16), x1, mod3, final_norm_w, 5, layer == depth - 1)
    return out
```
